```python
import math
import jax, jax.numpy as jnp
from jax import lax
import numpy as np

D_MODEL = 1024
BATCH = 2
SEQ = 8192
DEPTH = 2

GRID_W = 64
CTX_LEN = 256
HEAD_DIM = 64
NA_HEADS = D_MODEL // 2 // HEAD_DIM
NA_WIN_ROWS = 8
NA_WIN_COLS = 16
DIFF_HEAD_DIM = 64
DIFF_V_DIM = 2 * DIFF_HEAD_DIM
DIFF_HEADS = D_MODEL // 2 // DIFF_V_DIM
NA_WIDTH = NA_HEADS * HEAD_DIM
DIFF_QK_WIDTH = DIFF_HEADS * 2 * DIFF_HEAD_DIM
DIFF_V_WIDTH = DIFF_HEADS * DIFF_V_DIM
ATTN_IN_WIDTH = 3 * NA_WIDTH + 2 * DIFF_QK_WIDTH + DIFF_V_WIDTH
ATTN_OUT_WIDTH = NA_WIDTH + DIFF_V_WIDTH
ATTN_SPLITS = [NA_WIDTH, 2 * NA_WIDTH, 3 * NA_WIDTH,
               3 * NA_WIDTH + DIFF_QK_WIDTH, 3 * NA_WIDTH + 2 * DIFF_QK_WIDTH]
CONV_WIDTH = 3
D_FF = 2816
ROPE_THETA = 10000.0
ATTN_BLOCK = 128
N_MOD = 9
N_ATTN_LAYERS = (DEPTH + 1) // 2
N_CONV_LAYERS = DEPTH // 2
EPS = 1e-6
NEG_INF = -1e30

kernel_name = "hybrid_natten_diffattn_shortconv_macaron_dit"


def rmsnorm(x, g):
    x32 = x.astype(jnp.float32)
    y = x32 * lax.rsqrt(jnp.mean(x32 * x32, axis=-1, keepdims=True) + EPS)
    return (y * g.astype(jnp.float32)).astype(x.dtype)


def modulate(x, g, shift, scale):
    return rmsnorm(x, g) * (1 + scale) + shift


def swiglu(x, w_gate, w_up, w_down):
    return (jax.nn.silu(x @ w_gate) * (x @ w_up)) @ w_down


def macaron_ffn(xs, g, shift, scale, gate, w_gate, w_up, w_down):
    return xs + 0.5 * gate * swiglu(modulate(xs, g, shift, scale), w_gate, w_up, w_down)


def axial_rope(n):
    t = jnp.arange(n)
    row = (t // GRID_W).astype(jnp.float32)
    col = (t % GRID_W).astype(jnp.float32)
    n_freq = DIFF_HEAD_DIM // 4
    inv = ROPE_THETA ** (-jnp.arange(n_freq, dtype=jnp.float32) / n_freq)
    ang = jnp.concatenate([row[:, None] * inv, col[:, None] * inv], axis=-1)
    return jnp.cos(ang)[:, None, None, :], jnp.sin(ang)[:, None, None, :]


def apply_rope(x, cos, sin):
    x32 = x.astype(jnp.float32)
    x1, x2 = x32[..., 0::2], x32[..., 1::2]
    out = jnp.stack([x1 * cos - x2 * sin, x1 * sin + x2 * cos], axis=-1)
    return out.reshape(x.shape).astype(x.dtype)


def split_attn_proj(p):
    b, t, _ = p.shape
    qa, ka, va, qb, kb, vb = jnp.split(p, ATTN_SPLITS, axis=-1)
    na = lambda a: a.reshape(b, t, NA_HEADS, HEAD_DIM)
    dqk = lambda a: a.reshape(b, t, DIFF_HEADS, 2, DIFF_HEAD_DIM)
    return (na(qa), na(ka), na(va), dqk(qb), dqk(kb), vb.reshape(b, t, DIFF_HEADS, DIFF_V_DIM))


def dense_attend(q, k, v):
    s = jnp.einsum('bqhd,bkhd->bhqk', q * q.shape[-1] ** -0.5, k)
    p = jax.nn.softmax(s.astype(jnp.float32), axis=-1).astype(v.dtype)
    return jnp.einsum('bhqk,bkhd->bqhd', p, v)


def diff_attend(q, k, v, lam):
    s = jnp.einsum('bqhmd,bkhmd->bhmqk', q * q.shape[-1] ** -0.5, k)
    p = jax.nn.softmax(s.astype(jnp.float32), axis=-1)
    pd = (p[:, :, 0] - lam * p[:, :, 1]).astype(v.dtype)
    return jnp.einsum('bhqk,bkhe->bqhe', pd, v)


def neighbourhood_attention(q, k, v, k_ctx, v_ctx, rpb):
    b, n, h, d = q.shape
    rows = n // GRID_W
    kr = min(NA_WIN_ROWS, rows)
    qg = (q * d ** -0.5).reshape(b, rows, GRID_W, h, d)
    kg = k.reshape(b, rows, GRID_W, h, d)
    vg = v.reshape(b, rows, GRID_W, h, d)
    qc = jnp.arange(GRID_W)
    col_start = jnp.clip(qc - NA_WIN_COLS // 2, 0, GRID_W - NA_WIN_COLS)
    col_mask = (qc[None, :] >= col_start[:, None]) & (qc[None, :] < col_start[:, None] + NA_WIN_COLS)
    dc_idx = jnp.clip(qc[None, :] - qc[:, None] + NA_WIN_COLS - 1, 0, 2 * NA_WIN_COLS - 2)

    def row_block(r):
        start = jnp.clip(r - kr // 2, 0, rows - kr)
        q_r = lax.dynamic_index_in_dim(qg, r, axis=1, keepdims=False)
        k_band = lax.dynamic_slice_in_dim(kg, start, kr, axis=1)
        v_band = lax.dynamic_slice_in_dim(vg, start, kr, axis=1)
        s_band = jnp.einsum('bqhd,brkhd->bhqrk', q_r, k_band).astype(jnp.float32)
        dr_idx = start + jnp.arange(kr) - r + NA_WIN_ROWS - 1
        bias = rpb[:, dr_idx[None, :, None], dc_idx[:, None, :]].astype(jnp.float32)
        s_band = jnp.where(col_mask[:, None, :], s_band + bias, NEG_INF)
        s_ctx = jnp.einsum('bqhd,blhd->bhql', q_r, k_ctx).astype(jnp.float32)
        s = jnp.concatenate([s_band.reshape(b, h, GRID_W, kr * GRID_W), s_ctx], axis=-1)
        p = jax.nn.softmax(s, axis=-1).astype(v.dtype)
        p_band = p[..., :kr * GRID_W].reshape(b, h, GRID_W, kr, GRID_W)
        p_ctx = p[..., kr * GRID_W:]
        return (jnp.einsum('bhqrk,brkhd->bqhd', p_band, v_band)
                + jnp.einsum('bhql,blhd->bqhd', p_ctx, v_ctx))

    out = lax.map(row_block, jnp.arange(rows))
    return jnp.moveaxis(out, 0, 1).reshape(b, n, h, d)


def attn_mixer(xn, xn_ctx, w_in, w_out, rpb, lam_vec, subln_g, lam_init, cos, sin, ctx_out):
    b, n, _ = xn.shape
    qa, ka, va, qb, kb, vb = split_attn_proj(xn @ w_in)
    qa_c, ka_c, va_c, qb_c, kb_c, vb_c = split_attn_proj(xn_ctx @ w_in)
    qb, kb = apply_rope(qb, cos, sin), apply_rope(kb, cos, sin)
    l32 = lam_vec.astype(jnp.float32)
    lam = jnp.exp(jnp.sum(l32[0] * l32[1])) - jnp.exp(jnp.sum(l32[2] * l32[3])) + lam_init

    a_lat = neighbourhood_attention(qa, ka, va, ka_c, va_c, rpb)
    k_all = jnp.concatenate([kb, kb_c], axis=1)
    v_all = jnp.concatenate([vb, vb_c], axis=1)
    nb = n // ATTN_BLOCK
    q_blocks = jnp.moveaxis(qb.reshape(b, nb, ATTN_BLOCK, DIFF_HEADS, 2, DIFF_HEAD_DIM), 1, 0)
    d_lat = lax.map(lambda q_blk: diff_attend(q_blk, k_all, v_all, lam), q_blocks)
    d_lat = jnp.moveaxis(d_lat, 0, 1).reshape(b, n, DIFF_HEADS, DIFF_V_DIM)

    def merge(a, dd):
        t = a.shape[1]
        dd = rmsnorm(dd, subln_g) * (1.0 - lam_init)
        return jnp.concatenate([a.reshape(b, t, NA_WIDTH), dd.reshape(b, t, DIFF_V_WIDTH)], axis=-1) @ w_out

    y_lat = merge(a_lat, d_lat)
    if ctx_out:
        y_ctx = merge(dense_attend(qa_c, ka_c, va_c), diff_attend(qb_c, kb_c, vb_c, lam))
    else:
        y_ctx = None
    return y_lat, y_ctx


def conv_mixer(xn, w_in, w_out, conv_w):
    bg, cg, h = jnp.split(xn @ w_in, 3, axis=-1)
    u = cg * h
    v = lax.conv_general_dilated(u, conv_w[:, None, :].astype(u.dtype), window_strides=(1,),
                                 padding=((CONV_WIDTH // 2, CONV_WIDTH // 2),),
                                 dimension_numbers=('NWC', 'WIO', 'NWC'),
                                 feature_group_count=u.shape[-1])
    return (bg * v) @ w_out


def setup_inputs(seed: int = 0) -> dict:
    key = jax.random.key(seed)
    ks = jax.random.split(key, 20)
    nrm = lambda k, shape, s: jax.random.normal(k, shape, jnp.float32) * s
    D = D_MODEL
    return {
        "x": nrm(ks[0], (BATCH, SEQ, D), 1.0),
        "c": nrm(ks[1], (BATCH, D), 1.0),
        "ctx": nrm(ks[2], (BATCH, CTX_LEN, D), 1.0),
        "c_ctx": nrm(ks[3], (D,), 1.0),
        "mod_w": nrm(ks[4], (DEPTH, D, N_MOD * D), D ** -0.5),
        "mod_b": nrm(ks[5], (DEPTH, N_MOD * D), 0.01),
        "norm_g": 1.0 + nrm(ks[6], (DEPTH, 3, D), 0.1),
        "ffn_w_gate": nrm(ks[7], (DEPTH, 2, D, D_FF), D ** -0.5),
        "ffn_w_up": nrm(ks[8], (DEPTH, 2, D, D_FF), D ** -0.5),
        "ffn_w_down": nrm(ks[9], (DEPTH, 2, D_FF, D), D_FF ** -0.5),
        "attn_w_in": nrm(ks[10], (N_ATTN_LAYERS, D, ATTN_IN_WIDTH), D ** -0.5),
        "attn_w_out": nrm(ks[11], (N_ATTN_LAYERS, ATTN_OUT_WIDTH, D), ATTN_OUT_WIDTH ** -0.5),
        "na_rpb": nrm(ks[12], (N_ATTN_LAYERS, NA_HEADS, 2 * NA_WIN_ROWS - 1, 2 * NA_WIN_COLS - 1), 0.2),
        "diff_lambda": nrm(ks[13], (N_ATTN_LAYERS, 4, DIFF_HEAD_DIM), 0.1),
        "diff_subln_g": 1.0 + nrm(ks[14], (N_ATTN_LAYERS, DIFF_V_DIM), 0.1),
        "conv_w_in": nrm(ks[15], (N_CONV_LAYERS, D, 3 * D), D ** -0.5),
        "conv_w_out": nrm(ks[16], (N_CONV_LAYERS, D, D), D ** -0.5),
        "conv_w": nrm(ks[17], (N_CONV_LAYERS, CONV_WIDTH, D), CONV_WIDTH ** -0.5),
        "final_g": 1.0 + nrm(ks[18], (D,), 0.1),
    }


def reference(x, c, ctx, c_ctx, mod_w, mod_b, norm_g, ffn_w_gate, ffn_w_up, ffn_w_down,
              attn_w_in, attn_w_out, na_rpb, diff_lambda, diff_subln_g,
              conv_w_in, conv_w_out, conv_w, final_g):
    n = x.shape[1]
    cos, sin = axial_rope(n)
    x_lat, x_ctx = x, ctx
    for i in range(DEPTH):
        last = i == DEPTH - 1
        is_attn = i % 2 == 0
        ctx_live = is_attn or not last
        j = i // 2
        m = jnp.split((jax.nn.silu(c) @ mod_w[i] + mod_b[i])[:, None, :], N_MOD, axis=-1)
        if ctx_live:
            mc = jnp.split(jax.nn.silu(c_ctx) @ mod_w[i] + mod_b[i], N_MOD, axis=-1)

        x_lat = macaron_ffn(x_lat, norm_g[i, 0], m[0], m[1], m[2],
                            ffn_w_gate[i, 0], ffn_w_up[i, 0], ffn_w_down[i, 0])
        if ctx_live:
            x_ctx = macaron_ffn(x_ctx, norm_g[i, 0], mc[0], mc[1], mc[2],
                                ffn_w_gate[i, 0], ffn_w_up[i, 0], ffn_w_down[i, 0])

        xn = modulate(x_lat, norm_g[i, 1], m[3], m[4])
        if is_attn:
            xn_c = modulate(x_ctx, norm_g[i, 1], mc[3], mc[4])
            lam_init = 0.8 - 0.6 * math.exp(-0.3 * i)
            y, y_c = attn_mixer(xn, xn_c, attn_w_in[j], attn_w_out[j], na_rpb[j], diff_lambda[j],
                                diff_subln_g[j], lam_init, cos, sin, ctx_out=not last)
        else:
            y = conv_mixer(xn, conv_w_in[j], conv_w_out[j], conv_w[j])
            if not last:
                y_c = conv_mixer(modulate(x_ctx, norm_g[i, 1], mc[3], mc[4]),
                                 conv_w_in[j], conv_w_out[j], conv_w[j])
        x_lat = x_lat + m[5] * y
        if not last:
            x_ctx = x_ctx + mc[5] * y_c

        x_lat = macaron_ffn(x_lat, norm_g[i, 2], m[6], m[7], m[8],
                            ffn_w_gate[i, 1], ffn_w_up[i, 1], ffn_w_down[i, 1])
        if not last:
            x_ctx = macaron_ffn(x_ctx, norm_g[i, 2], mc[6], mc[7], mc[8],
                                ffn_w_gate[i, 1], ffn_w_up[i, 1], ffn_w_down[i, 1])
    return rmsnorm(x_lat, final_g)
```

```python
import functools
import math

import jax
import jax.numpy as jnp
from jax import lax
from jax.experimental import pallas as pl
from jax.experimental.pallas import tpu as pltpu

F32 = jnp.float32
BF16 = jnp.bfloat16

D_MODEL = 1024
GRID_W = 64
HEAD_DIM = 64
NA_HEADS = 8
NA_WIN_ROWS = 8
NA_WIN_COLS = 16
DIFF_HEADS = 4
DIFF_V_DIM = 128
HALF_WIDTH = 512
ATTN_IN_WIDTH = 6 * HALF_WIDTH
CONV_WIDTH = 3
D_FF = 2816
ROPE_THETA = 10000.0
N_MOD = 9
EPS = 1e-6
NEG_INF = -1e30

LANES = 128
ROW_TILE = 256
LAT_TILE = 512
NA_Q_ROWS = 4
NA_BAND_ROWS = 12
DIFF_Q_TILE = 512
DIFF_K_TILE = 256
FF_CHUNKS = ((0, 1024), (1024, 1024), (2048, 768))
VMEM_LIMIT = 56 * 1024 * 1024

NT_DIMS = (((1,), (1,)), ((), ()))


def _cparams(n_axes):
    return pltpu.CompilerParams(dimension_semantics=("arbitrary",) * n_axes,
                                vmem_limit_bytes=VMEM_LIMIT)


def _resident(shape, index_map):
    return pl.BlockSpec(shape, index_map, pipeline_mode=pl.Buffered(1))


def _modulate(x, g, shift, scale):
    ms = jnp.mean(x * x, axis=-1, keepdims=True)
    return x * lax.rsqrt(ms + EPS) * g * (1.0 + scale) + shift


def _silu(x):
    return x * (1.0 / (1.0 + jnp.exp(-x)))


def _mod_kernel(c_ref, w_ref, b_ref, o_ref):
    o_ref[0] = jnp.dot(_silu(c_ref[...]), w_ref[0], preferred_element_type=F32) + b_ref[0]


def _mod_vectors(cvec, mod_w, mod_b):
    depth, d, n = mod_w.shape
    bn = 1536
    return pl.pallas_call(
        _mod_kernel,
        grid=(depth, n // bn),
        in_specs=[pl.BlockSpec((8, d), lambda i, j: (0, 0)),
                  pl.BlockSpec((1, d, bn), lambda i, j: (i, 0, j)),
                  pl.BlockSpec((1, 1, bn), lambda i, j: (i, 0, j))],
        out_specs=pl.BlockSpec((1, 8, bn), lambda i, j: (i, 0, j)),
        out_shape=jax.ShapeDtypeStruct((depth, 8, n), F32),
        compiler_params=_cparams(2),
        name="adaln_vectors",
    )(cvec, mod_w, mod_b.reshape(depth, 1, n))


def _ffn_kernel(x_ref, mod_ref, g_ref, wg_ref, wu_ref, wd_ref, *rest, mod_base, final):
    o_ref = rest[-1]
    x = x_ref[...]
    shift = mod_ref[0, mod_base:mod_base + 1, :]
    scale = mod_ref[0, mod_base + 1:mod_base + 2, :]
    gate = mod_ref[0, mod_base + 2:mod_base + 3, :]
    xn = _modulate(x, g_ref[...], shift, scale).astype(BF16)
    y = None
    for lo, width in FF_CHUNKS:
        hg = jnp.dot(xn, wg_ref[:, lo:lo + width], preferred_element_type=F32)
        hu = jnp.dot(xn, wu_ref[:, lo:lo + width], preferred_element_type=F32)
        a = (_silu(hg) * hu).astype(BF16)
        part = jnp.dot(a, wd_ref[lo:lo + width, :], preferred_element_type=F32)
        y = part if y is None else y + part
    out = x + (0.5 * gate) * y
    if final:
        fg = rest[0][...]
        ms = jnp.mean(out * out, axis=-1, keepdims=True)
        out = out * lax.rsqrt(ms + EPS) * fg
    o_ref[...] = out


def _ffn(x, mods, group_of_tile, g, wg, wu, wd, *, tile, mod_base, final_g=None):
    rows, d = x.shape
    f = wg.shape[1]
    in_specs = [pl.BlockSpec((tile, d), lambda i: (i, 0)),
                pl.BlockSpec((1, N_MOD, d), lambda i: (group_of_tile(i), 0, 0)),
                _resident((1, d), lambda i: (0, 0)),
                _resident((d, f), lambda i: (0, 0)),
                _resident((d, f), lambda i: (0, 0)),
                _resident((f, d), lambda i: (0, 0))]
    args = [x, mods, g.reshape(1, d), wg, wu, wd]
    if final_g is not None:
        in_specs.append(_resident((1, d), lambda i: (0, 0)))
        args.append(final_g.reshape(1, d))
    return pl.pallas_call(
        functools.partial(_ffn_kernel, mod_base=mod_base, final=final_g is not None),
        grid=(rows // tile,),
        in_specs=in_specs,
        out_specs=pl.BlockSpec((tile, d), lambda i: (i, 0)),
        out_shape=jax.ShapeDtypeStruct((rows, d), F32),
        compiler_params=_cparams(1),
        name="macaron_ffn",
    )(*args)


def _rope(z, cos, sin_next, sin_prev):
    return z * cos + pltpu.roll(z, LANES - 1, 1) * sin_next + pltpu.roll(z, 1, 1) * sin_prev


def _attn_in_kernel(x_ref, mod_ref, g_ref, w_ref, cos_ref, sn_ref, sp_ref,
                    qa_ref, ka_ref, va_ref, qb_ref, kb_ref, vbt_ref):
    x = x_ref[...]
    xn = _modulate(x, g_ref[...], mod_ref[0, 3:4, :], mod_ref[0, 4:5, :]).astype(BF16)
    w = HALF_WIDTH
    scale = HEAD_DIM ** -0.5
    qa_ref[...] = (jnp.dot(xn, w_ref[:, 0:w], preferred_element_type=F32) * scale).astype(BF16)
    ka_ref[...] = jnp.dot(xn, w_ref[:, w:2 * w], preferred_element_type=F32).astype(BF16)
    va_ref[...] = jnp.dot(xn, w_ref[:, 2 * w:3 * w], preferred_element_type=F32).astype(BF16)
    cos, sn, sp = cos_ref[...], sn_ref[...], sp_ref[...]
    qb = jnp.dot(xn, w_ref[:, 3 * w:4 * w], preferred_element_type=F32)
    kb = jnp.dot(xn, w_ref[:, 4 * w:5 * w], preferred_element_type=F32)
    for h in range(DIFF_HEADS):
        sl = slice(h * LANES, (h + 1) * LANES)
        qb_ref[:, sl] = (_rope(qb[:, sl], cos, sn, sp) * scale).astype(BF16)
        kb_ref[:, sl] = _rope(kb[:, sl], cos, sn, sp).astype(BF16)
    vb = jnp.dot(xn, w_ref[:, 5 * w:6 * w], preferred_element_type=F32)
    vbt_ref[0] = vb.T.astype(BF16)


def _rope_tables(n_lat, n_ctx):
    t = jnp.arange(n_lat)
    row = (t // GRID_W).astype(F32)
    col = (t % GRID_W).astype(F32)
    n_freq = HEAD_DIM // 4
    inv = ROPE_THETA ** (-jnp.arange(n_freq, dtype=F32) / n_freq)
    ang = jnp.concatenate([row[:, None] * inv, col[:, None] * inv], axis=-1)
    cos = jnp.repeat(jnp.cos(ang), 2, axis=-1)
    sin = jnp.repeat(jnp.sin(ang), 2, axis=-1)
    even = (jnp.arange(HEAD_DIM) % 2 == 0)[None, :]
    sin_next = jnp.where(even, -sin, 0.0)
    sin_prev = jnp.where(even, 0.0, sin)
    pad = lambda a, v: jnp.concatenate([a, jnp.full((n_ctx, HEAD_DIM), v, F32)], axis=0)
    two = lambda a: jnp.concatenate([a, a], axis=-1)
    return two(pad(cos, 1.0)), two(pad(sin_next, 0.0)), two(pad(sin_prev, 0.0))


def _attn_in(xc, mods, g, w_in, tables, *, batch, tiles_per_batch):
    rows, d = xc.shape
    lat_tiles = tiles_per_batch - 1
    n_tiles = rows // ROW_TILE

    def group(i):
        return jnp.where(i % tiles_per_batch == lat_tiles, batch, i // tiles_per_batch)

    def lat_first(i):
        b, j = i // tiles_per_batch, i % tiles_per_batch
        return jnp.where(j == lat_tiles, batch * lat_tiles + b, b * lat_tiles + j)

    tab_spec = pl.BlockSpec((ROW_TILE, LANES), lambda i: (i % tiles_per_batch, 0))
    half = jax.ShapeDtypeStruct((rows, HALF_WIDTH), BF16)
    q_spec = pl.BlockSpec((ROW_TILE, HALF_WIDTH), lambda i: (lat_first(i), 0))
    kv_spec = pl.BlockSpec((ROW_TILE, HALF_WIDTH), lambda i: (i, 0))
    return pl.pallas_call(
        _attn_in_kernel,
        grid=(n_tiles,),
        in_specs=[pl.BlockSpec((ROW_TILE, d), lambda i: (i, 0)),
                  pl.BlockSpec((1, N_MOD, d), lambda i: (group(i), 0, 0)),
                  _resident((1, d), lambda i: (0, 0)),
                  _resident((d, ATTN_IN_WIDTH), lambda i: (0, 0)),
                  tab_spec, tab_spec, tab_spec],
        out_specs=[q_spec, kv_spec, kv_spec, q_spec, kv_spec,
                   pl.BlockSpec((1, HALF_WIDTH, ROW_TILE), lambda i: (i, 0, 0))],
        out_shape=[half, half, half, half, half,
                   jax.ShapeDtypeStruct((n_tiles, HALF_WIDTH, ROW_TILE), BF16)],
        compiler_params=_cparams(1),
        name="attn_in_proj",
    )(xc, mods, g.reshape(1, d), w_in, *tables)


def _na_bias_tables(rpb, rows):
    kr_win = min(NA_WIN_ROWS, rows)
    i = jnp.arange(NA_Q_ROWS)[:, None]
    j = jnp.arange(NA_BAND_ROWS)[None, :]
    qc = jnp.arange(GRID_W)
    col_start = jnp.clip(qc - NA_WIN_COLS // 2, 0, GRID_W - NA_WIN_COLS)
    col_ok = (qc[None, :] >= col_start[:, None]) & (qc[None, :] < col_start[:, None] + NA_WIN_COLS)
    dc = jnp.clip(qc[None, :] - qc[:, None] + NA_WIN_COLS - 1, 0, 2 * NA_WIN_COLS - 2)
    out = []
    for r0, ws in ((0, 0), (NA_Q_ROWS, 0), (rows - NA_Q_ROWS, rows - NA_BAND_ROWS)):
        qr = r0 + i
        kr = ws + j
        start = jnp.clip(qr - kr_win // 2, 0, rows - kr_win)
        row_ok = (kr >= start) & (kr < start + kr_win)
        dr = jnp.clip(kr - qr + NA_WIN_ROWS - 1, 0, 2 * NA_WIN_ROWS - 2)
        b = rpb[:, dr[:, None, :, None], dc[None, :, None, :]].astype(F32)
        ok = row_ok[:, None, :, None] & col_ok[None, :, None, :]
        b = jnp.where(ok[None], b, NEG_INF)
        out.append(b.reshape(rpb.shape[0], NA_Q_ROWS * GRID_W, NA_BAND_ROWS * GRID_W))
    return jnp.stack(out)


def _na_kernel(q_ref, k_ref, v_ref, bias_ref, o_ref, *, n_lat, n_ctx, rows):
    g = pl.program_id(1)
    ws = jnp.clip(NA_Q_ROWS * g - NA_WIN_ROWS // 2, 0, rows - NA_BAND_ROWS)
    start = pl.multiple_of(ws * GRID_W, GRID_W)
    band = NA_BAND_ROWS * GRID_W
    tq = NA_Q_ROWS * GRID_W
    low = lax.broadcasted_iota(jnp.int32, (tq, LANES), 1) < HEAD_DIM
    for p in range(NA_HEADS // 2):
        sl = slice(p * LANES, (p + 1) * LANES)
        q = q_ref[:, sl]
        kb = k_ref[pl.ds(start, band), sl]
        vb = v_ref[pl.ds(start, band), sl]
        kc = k_ref[n_lat:n_lat + n_ctx, sl]
        vc = v_ref[n_lat:n_lat + n_ctx, sl]
        outs = []
        for e in range(2):
            qm = jnp.where(low if e == 0 else jnp.logical_not(low), q, jnp.zeros_like(q))
            sb = lax.dot_general(qm, kb, NT_DIMS, preferred_element_type=F32) + bias_ref[0, 2 * p + e]
            sc = lax.dot_general(qm, kc, NT_DIMS, preferred_element_type=F32)
            m = jnp.maximum(jnp.max(sb, axis=-1, keepdims=True), jnp.max(sc, axis=-1, keepdims=True))
            pb = jnp.exp(sb - m)
            pc = jnp.exp(sc - m)
            l = jnp.sum(pb, axis=-1, keepdims=True) + jnp.sum(pc, axis=-1, keepdims=True)
            o = (jnp.dot(pb.astype(BF16), vb, preferred_element_type=F32)
                 + jnp.dot(pc.astype(BF16), vc, preferred_element_type=F32))
            outs.append(o * (1.0 / l))
        o_ref[:, sl] = jnp.where(low, outs[0], outs[1]).astype(BF16)


def _na_attention(qa, ka, va, bias, *, batch, n_lat, n_ctx):
    rows = n_lat // GRID_W
    groups = rows // NA_Q_ROWS
    tq = NA_Q_ROWS * GRID_W
    per_batch = n_lat + n_ctx

    def variant(g):
        return jnp.where(g == 0, 0, jnp.where(g == groups - 1, 2, 1))

    kv_spec = _resident((per_batch, HALF_WIDTH), lambda b, g: (b, 0))
    return pl.pallas_call(
        functools.partial(_na_kernel, n_lat=n_lat, n_ctx=n_ctx, rows=rows),
        grid=(batch, groups),
        in_specs=[pl.BlockSpec((tq, HALF_WIDTH), lambda b, g: (b * groups + g, 0)),
                  kv_spec, kv_spec,
                  pl.BlockSpec((1, NA_HEADS, tq, NA_BAND_ROWS * GRID_W), lambda b, g: (variant(g), 0, 0, 0))],
        out_specs=pl.BlockSpec((tq, HALF_WIDTH), lambda b, g: (b * groups + g, 0)),
        out_shape=jax.ShapeDtypeStruct((batch * n_lat, HALF_WIDTH), BF16),
        compiler_params=_cparams(2),
        name="neighbourhood_attention",
    )(qa, ka, va, bias)


def _diff_kernel(q_ref, k_ref, vt_ref, lam_ref, g_ref, o_ref, m_ref, l_ref, acc_ref, *, k_tiles, lam_init):
    tq = q_ref.shape[0]
    q = q_ref[...]
    low = lax.broadcasted_iota(jnp.int32, (tq, LANES), 1) < HEAD_DIM
    zero = jnp.zeros_like(q)
    qs = (jnp.where(low, q, zero), jnp.where(low, zero, q))
    m_ref[...] = jnp.full(m_ref.shape, -jnp.inf, F32)
    l_ref[...] = jnp.zeros(l_ref.shape, F32)
    acc_ref[...] = jnp.zeros(acc_ref.shape, F32)

    def body(c, carry):
        k = k_ref[pl.ds(pl.multiple_of(c * DIFF_K_TILE, DIFF_K_TILE), DIFF_K_TILE), :]
        vt = vt_ref[c]
        for e in range(2):
            s = lax.dot_general(k, qs[e], NT_DIMS, preferred_element_type=F32)
            m_old = m_ref[e]
            m_new = jnp.maximum(m_old, jnp.max(s, axis=0, keepdims=True))
            alpha = jnp.exp(m_old - m_new)
            p = jnp.exp(s - m_new)
            l_ref[e] = alpha * l_ref[e] + jnp.sum(p, axis=0, keepdims=True)
            acc_ref[e] = alpha * acc_ref[e] + jnp.dot(vt, p.astype(BF16), preferred_element_type=F32)
            m_ref[e] = m_new
        return carry

    lax.fori_loop(0, k_tiles, body, 0)
    lv = lam_ref[...]
    lam = (jnp.exp(jnp.sum(lv[0:1] * lv[1:2], axis=-1, keepdims=True))
           - jnp.exp(jnp.sum(lv[2:3] * lv[3:4], axis=-1, keepdims=True)) + lam_init)
    d = acc_ref[0] * (1.0 / l_ref[0]) - lam * (acc_ref[1] * (1.0 / l_ref[1]))
    ms = jnp.mean(d * d, axis=0, keepdims=True)
    d = d * lax.rsqrt(ms + EPS) * g_ref[...] * (1.0 - lam_init)
    o_ref[...] = d.T.astype(BF16)


def _diff_attention(qb, kb, vbt, lam_vec, subln_g, *, batch, n_q, q_row0, k_tile0, k_tiles,
                    tiles_per_batch, q_tile, lam_init):
    q_tiles = n_q // q_tile
    q_blk0 = q_row0 // q_tile
    if k_tile0 == 0:
        k_spec = _resident((k_tiles * DIFF_K_TILE, LANES), lambda b, h, i: (b, h))
    else:
        k_spec = _resident((k_tiles * DIFF_K_TILE, LANES), lambda b, h, i: (b * tiles_per_batch + k_tile0, h))
    return pl.pallas_call(
        functools.partial(_diff_kernel, k_tiles=k_tiles, lam_init=lam_init),
        grid=(batch, DIFF_HEADS, q_tiles),
        in_specs=[pl.BlockSpec((q_tile, LANES), lambda b, h, i: (q_blk0 + b * q_tiles + i, h)),
                  k_spec,
                  _resident((k_tiles, LANES, DIFF_K_TILE),
                            lambda b, h, i: ((b * tiles_per_batch + k_tile0) // k_tiles, h, 0)),
                  _resident((4, HEAD_DIM), lambda b, h, i: (0, 0)),
                  _resident((DIFF_V_DIM, 1), lambda b, h, i: (0, 0))],
        out_specs=pl.BlockSpec((q_tile, LANES), lambda b, h, i: (b * q_tiles + i, h)),
        out_shape=jax.ShapeDtypeStruct((batch * n_q, HALF_WIDTH), BF16),
        scratch_shapes=[pltpu.VMEM((2, 1, q_tile), F32),
                        pltpu.VMEM((2, 1, q_tile), F32),
                        pltpu.VMEM((2, DIFF_V_DIM, q_tile), F32)],
        compiler_params=_cparams(3),
        name="differential_attention",
    )(qb, kb, vbt, lam_vec, subln_g.reshape(DIFF_V_DIM, 1))


def _attn_out_kernel(x_ref, a_ref, d_ref, mod_ref, w_ref, o_ref):
    y = (jnp.dot(a_ref[...], w_ref[0:HALF_WIDTH, :], preferred_element_type=F32)
         + jnp.dot(d_ref[...], w_ref[HALF_WIDTH:2 * HALF_WIDTH, :], preferred_element_type=F32))
    o_ref[...] = x_ref[...] + mod_ref[0, 5:6, :] * y


def _attn_out(x, a, dd, mods, w_out, *, x_tile_of, group_of_tile, n_tiles):
    d = x.shape[1]
    return pl.pallas_call(
        _attn_out_kernel,
        grid=(n_tiles,),
        in_specs=[pl.BlockSpec((ROW_TILE, d), lambda i: (x_tile_of(i), 0)),
                  pl.BlockSpec((ROW_TILE, HALF_WIDTH), lambda i: (i, 0)),
                  pl.BlockSpec((ROW_TILE, HALF_WIDTH), lambda i: (i, 0)),
                  pl.BlockSpec((1, N_MOD, d), lambda i: (group_of_tile(i), 0, 0)),
                  _resident((2 * HALF_WIDTH, d), lambda i: (0, 0))],
        out_specs=pl.BlockSpec((ROW_TILE, d), lambda i: (i, 0)),
        out_shape=jax.ShapeDtypeStruct((n_tiles * ROW_TILE, d), F32),
        compiler_params=_cparams(1),
        name="attn_out_proj",
    )(x, a, dd, mods, w_out)


def _dense_kernel(q_ref, k_ref, v_ref, o_ref):
    tq = q_ref.shape[0]
    low = lax.broadcasted_iota(jnp.int32, (tq, LANES), 1) < HEAD_DIM
    for p in range(NA_HEADS // 2):
        sl = slice(p * LANES, (p + 1) * LANES)
        q, k, v = q_ref[:, sl], k_ref[:, sl], v_ref[:, sl]
        outs = []
        for e in range(2):
            qm = jnp.where(low if e == 0 else jnp.logical_not(low), q, jnp.zeros_like(q))
            s = lax.dot_general(qm, k, NT_DIMS, preferred_element_type=F32)
            pr = jnp.exp(s - jnp.max(s, axis=-1, keepdims=True))
            o = jnp.dot(pr.astype(BF16), v, preferred_element_type=F32)
            outs.append(o * (1.0 / jnp.sum(pr, axis=-1, keepdims=True)))
        o_ref[:, sl] = jnp.where(low, outs[0], outs[1]).astype(BF16)


def _dense_attention(qa, ka, va, *, batch, q_row0, k_tile0, tiles_per_batch):
    q_blk0 = q_row0 // ROW_TILE
    kv_spec = pl.BlockSpec((ROW_TILE, HALF_WIDTH), lambda b: (b * tiles_per_batch + k_tile0, 0))
    return pl.pallas_call(
        _dense_kernel,
        grid=(batch,),
        in_specs=[pl.BlockSpec((ROW_TILE, HALF_WIDTH), lambda b: (q_blk0 + b, 0)), kv_spec, kv_spec],
        out_specs=pl.BlockSpec((ROW_TILE, HALF_WIDTH), lambda b: (b, 0)),
        out_shape=jax.ShapeDtypeStruct((batch * ROW_TILE, HALF_WIDTH), BF16),
        compiler_params=_cparams(1),
        name="context_attention",
    )(qa, ka, va)


def _conv_kernel(x_ref, xp_ref, xs_ref, mod_ref, g_ref, w_in_ref, cw_ref, w_out_ref, o_ref, *, tiles_per_seq):
    i = pl.program_id(0)
    tile, d = x_ref.shape
    halo = xp_ref.shape[0]
    x = x_ref[...]
    x_ext = jnp.concatenate([xp_ref[...], x, xs_ref[...]], axis=0)
    xn = _modulate(x_ext, g_ref[...], mod_ref[0, 3:4, :], mod_ref[0, 4:5, :]).astype(BF16)
    cg = jnp.dot(xn, w_in_ref[:, d:2 * d], preferred_element_type=F32)
    hh = jnp.dot(xn, w_in_ref[:, 2 * d:3 * d], preferred_element_type=F32)
    u = cg * hh
    n_ext = tile + 2 * halo
    ridx = lax.broadcasted_iota(jnp.int32, (n_ext, 1), 0)
    first = (i % tiles_per_seq) == 0
    last = (i % tiles_per_seq) == tiles_per_seq - 1
    u = jnp.where(jnp.logical_or(jnp.logical_and(first, ridx < halo),
                                 jnp.logical_and(last, ridx >= halo + tile)), 0.0, u)
    cw = cw_ref[...]
    u_prev = pltpu.roll(u, 1, 0)[halo:halo + tile]
    u_next = pltpu.roll(u, n_ext - 1, 0)[halo:halo + tile]
    v = u_prev * cw[0:1] + u[halo:halo + tile] * cw[1:2] + u_next * cw[2:3]
    bg = jnp.dot(xn[halo:halo + tile], w_in_ref[:, 0:d], preferred_element_type=F32)
    y = jnp.dot((bg * v).astype(BF16), w_out_ref[...], preferred_element_type=F32)
    o_ref[...] = x + mod_ref[0, 5:6, :] * y


def _conv_mixer(x, mods, group_of_tile, g, w_in, conv_w, w_out, *, tile, seq):
    rows, d = x.shape
    halo = 8
    tiles_per_seq = seq // tile
    hb = tile // halo
    n_hblocks = rows // halo
    return pl.pallas_call(
        functools.partial(_conv_kernel, tiles_per_seq=tiles_per_seq),
        grid=(rows // tile,),
        in_specs=[pl.BlockSpec((tile, d), lambda i: (i, 0)),
                  pl.BlockSpec((halo, d), lambda i: (jnp.maximum(i * hb - 1, 0), 0)),
                  pl.BlockSpec((halo, d), lambda i: (jnp.minimum((i + 1) * hb, n_hblocks - 1), 0)),
                  pl.BlockSpec((1, N_MOD, d), lambda i: (group_of_tile(i), 0, 0)),
                  _resident((1, d), lambda i: (0, 0)),
                  _resident((d, 3 * d), lambda i: (0, 0)),
                  _resident((CONV_WIDTH, d), lambda i: (0, 0)),
                  _resident((d, d), lambda i: (0, 0))],
        out_specs=pl.BlockSpec((tile, d), lambda i: (i, 0)),
        out_shape=jax.ShapeDtypeStruct((rows, d), F32),
        compiler_params=_cparams(1),
        name="conv_mixer",
    )(x, x, x, mods, g.reshape(1, d), w_in, conv_w, w_out)


def kernel(x, c, ctx, c_ctx, mod_w, mod_b, norm_g, ffn_w_gate, ffn_w_up, ffn_w_down, attn_w_in, attn_w_out,
           na_rpb, diff_lambda, diff_subln_g, conv_w_in, conv_w_out, conv_w, final_g):
    batch, n_lat, d = x.shape
    n_ctx = ctx.shape[1]
    depth = mod_w.shape[0]
    assert n_ctx == ROW_TILE and n_lat % LAT_TILE == 0 and depth == 2 and batch + 1 <= 8
    tiles_per_batch = (n_lat + n_ctx) // ROW_TILE
    lat_tiles = n_lat // ROW_TILE
    bf = lambda a: a.astype(BF16)

    cvec = jnp.concatenate([c, c_ctx[None, :], jnp.zeros((8 - batch - 1, d), F32)], axis=0)
    mods_all = _mod_vectors(cvec, mod_w, mod_b).reshape(depth, 8, N_MOD, d)

    def comb_group(i):
        return jnp.where(i % tiles_per_batch == lat_tiles, batch, i // tiles_per_batch)

    mods = mods_all[0]
    xc = jnp.concatenate([x, ctx], axis=1).reshape(batch * (n_lat + n_ctx), d)
    xc = _ffn(xc, mods, comb_group, norm_g[0, 0], bf(ffn_w_gate[0, 0]), bf(ffn_w_up[0, 0]),
              bf(ffn_w_down[0, 0]), tile=ROW_TILE, mod_base=0)
    qa, ka, va, qb, kb, vbt = _attn_in(xc, mods, norm_g[0, 1], bf(attn_w_in[0]),
                                       _rope_tables(n_lat, n_ctx), batch=batch, tiles_per_batch=tiles_per_batch)
    lam_init = 0.8 - 0.6 * math.exp(-0.3 * 0)
    a_lat = _na_attention(qa, ka, va, _na_bias_tables(na_rpb[0], n_lat // GRID_W),
                          batch=batch, n_lat=n_lat, n_ctx=n_ctx)
    d_lat = _diff_attention(qb, kb, vbt, diff_lambda[0], diff_subln_g[0], batch=batch, n_q=n_lat, q_row0=0,
                            k_tile0=0, k_tiles=tiles_per_batch, tiles_per_batch=tiles_per_batch,
                            q_tile=DIFF_Q_TILE, lam_init=lam_init)
    w_out = bf(attn_w_out[0])
    x_lat = _attn_out(xc, a_lat, d_lat, mods, w_out, n_tiles=batch * lat_tiles,
                      x_tile_of=lambda i: (i // lat_tiles) * tiles_per_batch + i % lat_tiles,
                      group_of_tile=lambda i: i // lat_tiles)
    lat_group = lambda i: i // (n_lat // LAT_TILE)
    x_lat = _ffn(x_lat, mods, lat_group, norm_g[0, 2], bf(ffn_w_gate[0, 1]), bf(ffn_w_up[0, 1]),
                 bf(ffn_w_down[0, 1]), tile=LAT_TILE, mod_base=6)

    a_ctx = _dense_attention(qa, ka, va, batch=batch, q_row0=batch * n_lat, k_tile0=lat_tiles,
                             tiles_per_batch=tiles_per_batch)
    d_ctx = _diff_attention(qb, kb, vbt, diff_lambda[0], diff_subln_g[0], batch=batch, n_q=n_ctx,
                            q_row0=batch * n_lat, k_tile0=lat_tiles, k_tiles=1,
                            tiles_per_batch=tiles_per_batch, q_tile=ROW_TILE, lam_init=lam_init)
    x_ctx = _attn_out(xc, a_ctx, d_ctx, mods, w_out, n_tiles=batch,
                      x_tile_of=lambda i: i * tiles_per_batch + lat_tiles, group_of_tile=lambda i: batch)
    x_ctx = _ffn(x_ctx, mods, lambda i: batch, norm_g[0, 2], bf(ffn_w_gate[0, 1]), bf(ffn_w_up[0, 1]),
                 bf(ffn_w_down[0, 1]), tile=ROW_TILE, mod_base=6)
    del x_ctx

    mods = mods_all[1]
    x_lat = _ffn(x_lat, mods, lat_group, norm_g[1, 0], bf(ffn_w_gate[1, 0]), bf(ffn_w_up[1, 0]),
                 bf(ffn_w_down[1, 0]), tile=LAT_TILE, mod_base=0)
    x_lat = _conv_mixer(x_lat, mods, lat_group, norm_g[1, 1], bf(conv_w_in[0]), conv_w[0], bf(conv_w_out[0]),
                        tile=LAT_TILE, seq=n_lat)
    x_lat = _ffn(x_lat, mods, lat_group, norm_g[1, 2], bf(ffn_w_gate[1, 1]), bf(ffn_w_up[1, 1]),
                 bf(ffn_w_down[1, 1]), tile=LAT_TILE, mod_base=6, final_g=final_g)
    return x_lat.reshape(batch, n_lat, d)
```

```python
import functools
import math

import jax
import jax.numpy as jnp
import numpy as np
from jax import lax
from jax.experimental import pallas as pl
from jax.experimental.pallas import tpu as pltpu

F32 = jnp.float32
BF16 = jnp.bfloat16

D_MODEL = 1024
GRID_W = 64
HEAD_DIM = 64
NA_HEADS = 8
NA_WIN_ROWS = 8
NA_WIN_COLS = 16
DIFF_HEADS = 4
DIFF_V_DIM = 128
DIFF_V_ROWS = 144
LOG2_E = math.log2(math.e)
HALF_WIDTH = 512
ATTN_IN_WIDTH = 6 * HALF_WIDTH
CONV_WIDTH = 3
D_FF = 2816
ROPE_THETA = 10000.0
N_MOD = 9
EPS = 1e-6
NEG_INF = -1e30

LANES = 128
ROW_TILE = 256
LAT_TILE = 512
NA_Q_ROWS = 4
NA_BAND_ROWS = 12
DIFF_Q_TILE = 512
DIFF_K_TILE = 256
DIFF_CHUNK_TILES = 3
FF_CHUNKS = ((0, 1024), (1024, 1024), (2048, 768))
VMEM_LIMIT = 56 * 1024 * 1024

NT_DIMS = (((1,), (1,)), ((), ()))


def _cparams(n_axes):
    return pltpu.CompilerParams(dimension_semantics=("arbitrary",) * n_axes,
                                vmem_limit_bytes=VMEM_LIMIT)


def _resident(shape, index_map):
    return pl.BlockSpec(shape, index_map, pipeline_mode=pl.Buffered(1))


def _modulate(x, g, shift, scale):
    ms = jnp.mean(x * x, axis=-1, keepdims=True)
    return x * lax.rsqrt(ms + EPS) * g * (1.0 + scale) + shift


def _silu(x):
    return x * (1.0 / (1.0 + jnp.exp(-x)))


def _mod_kernel(c_ref, w_ref, b_ref, o_ref):
    o_ref[0] = jnp.dot(_silu(c_ref[...]), w_ref[0], preferred_element_type=F32) + b_ref[0]


def _mod_vectors(cvec, mod_w, mod_b):
    depth, d, n = mod_w.shape
    bn = 1536
    return pl.pallas_call(
        _mod_kernel,
        grid=(depth, n // bn),
        in_specs=[pl.BlockSpec((8, d), lambda i, j: (0, 0)),
                  pl.BlockSpec((1, d, bn), lambda i, j: (i, 0, j)),
                  pl.BlockSpec((1, 1, bn), lambda i, j: (i, 0, j))],
        out_specs=pl.BlockSpec((1, 8, bn), lambda i, j: (i, 0, j)),
        out_shape=jax.ShapeDtypeStruct((depth, 8, n), F32),
        compiler_params=_cparams(2),
        name="adaln_vectors",
    )(cvec, mod_w, mod_b.reshape(depth, 1, n))


def _ffn_kernel(x_ref, mod_ref, g_ref, wg_ref, wu_ref, wd_ref, *rest, mod_base, final):
    o_ref = rest[-1]
    x = x_ref[...]
    shift = mod_ref[0, mod_base:mod_base + 1, :]
    scale = mod_ref[0, mod_base + 1:mod_base + 2, :]
    gate = mod_ref[0, mod_base + 2:mod_base + 3, :]
    xn = _modulate(x, g_ref[...], shift, scale).astype(BF16)
    y = None
    for lo, width in FF_CHUNKS:
        hg = jnp.dot(xn, wg_ref[:, lo:lo + width], preferred_element_type=F32)
        hu = jnp.dot(xn, wu_ref[:, lo:lo + width], preferred_element_type=F32)
        a = (_silu(hg) * hu).astype(BF16)
        part = jnp.dot(a, wd_ref[lo:lo + width, :], preferred_element_type=F32)
        y = part if y is None else y + part
    out = x + (0.5 * gate) * y
    if final:
        fg = rest[0][...]
        ms = jnp.mean(out * out, axis=-1, keepdims=True)
        out = out * lax.rsqrt(ms + EPS) * fg
    o_ref[...] = out


def _ffn(x, mods, group_of_tile, g, wg, wu, wd, *, tile, mod_base, final_g=None):
    rows, d = x.shape
    f = wg.shape[1]
    in_specs = [pl.BlockSpec((tile, d), lambda i: (i, 0)),
                pl.BlockSpec((1, N_MOD, d), lambda i: (group_of_tile(i), 0, 0)),
                _resident((1, d), lambda i: (0, 0)),
                _resident((d, f), lambda i: (0, 0)),
                _resident((d, f), lambda i: (0, 0)),
                _resident((f, d), lambda i: (0, 0))]
    args = [x, mods, g.reshape(1, d), wg, wu, wd]
    if final_g is not None:
        in_specs.append(_resident((1, d), lambda i: (0, 0)))
        args.append(final_g.reshape(1, d))
    return pl.pallas_call(
        functools.partial(_ffn_kernel, mod_base=mod_base, final=final_g is not None),
        grid=(rows // tile,),
        in_specs=in_specs,
        out_specs=pl.BlockSpec((tile, d), lambda i: (i, 0)),
        out_shape=jax.ShapeDtypeStruct((rows, d), F32),
        compiler_params=_cparams(1),
        name="macaron_ffn",
    )(*args)


def _rope(z, cos, sin_next, sin_prev):
    return z * cos + pltpu.roll(z, LANES - 1, 1) * sin_next + pltpu.roll(z, 1, 1) * sin_prev


def _attn_in_kernel(x_ref, mod_ref, g_ref, w_ref, cos_ref, sn_ref, sp_ref,
                    qa_ref, ka_ref, va_ref, qb_ref, kb_ref, vbt_ref):
    x = x_ref[...]
    xn = _modulate(x, g_ref[...], mod_ref[0, 3:4, :], mod_ref[0, 4:5, :]).astype(BF16)
    w = HALF_WIDTH
    scale = HEAD_DIM ** -0.5
    qa_ref[...] = (jnp.dot(xn, w_ref[:, 0:w], preferred_element_type=F32) * scale).astype(BF16)
    ka_ref[...] = jnp.dot(xn, w_ref[:, w:2 * w], preferred_element_type=F32).astype(BF16)
    va_ref[...] = jnp.dot(xn, w_ref[:, 2 * w:3 * w], preferred_element_type=F32).astype(BF16)
    cos, sn, sp = cos_ref[...], sn_ref[...], sp_ref[...]
    qb = jnp.dot(xn, w_ref[:, 3 * w:4 * w], preferred_element_type=F32)
    kb = jnp.dot(xn, w_ref[:, 4 * w:5 * w], preferred_element_type=F32)
    for h in range(DIFF_HEADS):
        sl = slice(h * LANES, (h + 1) * LANES)
        qb_ref[:, sl] = (_rope(qb[:, sl], cos, sn, sp) * (scale * LOG2_E)).astype(BF16)
        kb_ref[:, sl] = _rope(kb[:, sl], cos, sn, sp).astype(BF16)
    vbt = jnp.dot(xn, w_ref[:, 5 * w:6 * w], preferred_element_type=F32).T.astype(BF16)
    pad_rows = DIFF_V_ROWS - DIFF_V_DIM
    ones_row = (lax.broadcasted_iota(jnp.int32, (pad_rows, vbt.shape[1]), 0) == 0).astype(BF16)
    for h in range(DIFF_HEADS):
        vbt_ref[0, h * DIFF_V_ROWS:h * DIFF_V_ROWS + DIFF_V_DIM, :] = vbt[h * DIFF_V_DIM:(h + 1) * DIFF_V_DIM]
        vbt_ref[0, h * DIFF_V_ROWS + DIFF_V_DIM:(h + 1) * DIFF_V_ROWS, :] = ones_row


def _rope_tables(n_lat, n_ctx):
    t = jnp.arange(n_lat)
    row = (t // GRID_W).astype(F32)
    col = (t % GRID_W).astype(F32)
    n_freq = HEAD_DIM // 4
    inv = ROPE_THETA ** (-jnp.arange(n_freq, dtype=F32) / n_freq)
    ang = jnp.concatenate([row[:, None] * inv, col[:, None] * inv], axis=-1)
    cos = jnp.repeat(jnp.cos(ang), 2, axis=-1)
    sin = jnp.repeat(jnp.sin(ang), 2, axis=-1)
    even = (jnp.arange(HEAD_DIM) % 2 == 0)[None, :]
    sin_next = jnp.where(even, -sin, 0.0)
    sin_prev = jnp.where(even, 0.0, sin)
    pad = lambda a, v: jnp.concatenate([a, jnp.full((n_ctx, HEAD_DIM), v, F32)], axis=0)
    two = lambda a: jnp.concatenate([a, a], axis=-1)
    return two(pad(cos, 1.0)), two(pad(sin_next, 0.0)), two(pad(sin_prev, 0.0))


def _attn_in(xc, mods, g, w_in, tables, *, batch, tiles_per_batch):
    rows, d = xc.shape
    lat_tiles = tiles_per_batch - 1
    n_tiles = rows // ROW_TILE

    def group(i):
        return jnp.where(i % tiles_per_batch == lat_tiles, batch, i // tiles_per_batch)

    def lat_first(i):
        b, j = i // tiles_per_batch, i % tiles_per_batch
        return jnp.where(j == lat_tiles, batch * lat_tiles + b, b * lat_tiles + j)

    tab_spec = pl.BlockSpec((ROW_TILE, LANES), lambda i: (i % tiles_per_batch, 0))
    half = jax.ShapeDtypeStruct((rows, HALF_WIDTH), BF16)
    q_spec = pl.BlockSpec((ROW_TILE, HALF_WIDTH), lambda i: (lat_first(i), 0))
    kv_spec = pl.BlockSpec((ROW_TILE, HALF_WIDTH), lambda i: (i, 0))
    return pl.pallas_call(
        _attn_in_kernel,
        grid=(n_tiles,),
        in_specs=[pl.BlockSpec((ROW_TILE, d), lambda i: (i, 0)),
                  pl.BlockSpec((1, N_MOD, d), lambda i: (group(i), 0, 0)),
                  _resident((1, d), lambda i: (0, 0)),
                  _resident((d, ATTN_IN_WIDTH), lambda i: (0, 0)),
                  tab_spec, tab_spec, tab_spec],
        out_specs=[q_spec, kv_spec, kv_spec, q_spec, kv_spec,
                   pl.BlockSpec((1, DIFF_HEADS * DIFF_V_ROWS, ROW_TILE), lambda i: (i, 0, 0))],
        out_shape=[half, half, half, half, half,
                   jax.ShapeDtypeStruct((n_tiles, DIFF_HEADS * DIFF_V_ROWS, ROW_TILE), BF16)],
        compiler_params=_cparams(1),
        name="attn_in_proj",
    )(xc, mods, g.reshape(1, d), w_in, *tables)


def _na_bias_tables(rpb, rows):
    heads = rpb.shape[0]
    kr_win = min(NA_WIN_ROWS, rows)
    qc = np.arange(GRID_W)
    col_start = np.clip(qc - NA_WIN_COLS // 2, 0, GRID_W - NA_WIN_COLS)
    col_ok = (qc[None, :] >= col_start[:, None]) & (qc[None, :] < col_start[:, None] + NA_WIN_COLS)
    dc = np.clip(qc[None, :] - qc[:, None] + NA_WIN_COLS - 1, 0, 2 * NA_WIN_COLS - 2)
    rpb = rpb.astype(F32)
    t = jnp.full((heads, 2 * NA_WIN_ROWS - 1, GRID_W, GRID_W), NEG_INF, F32)
    for cidx in range(2 * NA_WIN_COLS - 1):
        t = jnp.where((col_ok & (dc == cidx))[None, None], rpb[:, :, cidx][:, :, None, None], t)
    neg = jnp.full((heads, GRID_W, GRID_W), NEG_INF, F32)
    out = []
    for r0, ws in ((0, 0), (NA_Q_ROWS, 0), (rows - NA_Q_ROWS, rows - NA_BAND_ROWS)):
        block_rows = []
        for i in range(NA_Q_ROWS):
            qr = r0 + i
            start = min(max(qr - kr_win // 2, 0), rows - kr_win)
            blocks = [t[:, kr - qr + NA_WIN_ROWS - 1] if start <= kr < start + kr_win else neg
                      for kr in range(ws, ws + NA_BAND_ROWS)]
            block_rows.append(jnp.concatenate(blocks, axis=-1))
        out.append(jnp.concatenate(block_rows, axis=1))
    return jnp.stack(out)


def _na_kernel(q_ref, k_ref, v_ref, bias_ref, o_ref, *, n_lat, n_ctx, rows):
    g = pl.program_id(1)
    ws = jnp.clip(NA_Q_ROWS * g - NA_WIN_ROWS // 2, 0, rows - NA_BAND_ROWS)
    start = pl.multiple_of(ws * GRID_W, GRID_W)
    band = NA_BAND_ROWS * GRID_W
    tq = NA_Q_ROWS * GRID_W
    low = lax.broadcasted_iota(jnp.int32, (tq, LANES), 1) < HEAD_DIM
    for p in range(NA_HEADS // 2):
        sl = slice(p * LANES, (p + 1) * LANES)
        q = q_ref[:, sl]
        kb = k_ref[pl.ds(start, band), sl]
        vb = v_ref[pl.ds(start, band), sl]
        kc = k_ref[n_lat:n_lat + n_ctx, sl]
        vc = v_ref[n_lat:n_lat + n_ctx, sl]
        outs = []
        for e in range(2):
            qm = jnp.where(low if e == 0 else jnp.logical_not(low), q, jnp.zeros_like(q))
            sb = lax.dot_general(qm, kb, NT_DIMS, preferred_element_type=F32) + bias_ref[0, 2 * p + e]
            sc = lax.dot_general(qm, kc, NT_DIMS, preferred_element_type=F32)
            m = jnp.maximum(jnp.max(sb, axis=-1, keepdims=True), jnp.max(sc, axis=-1, keepdims=True))
            pb = jnp.exp(sb - m)
            pc = jnp.exp(sc - m)
            l = jnp.sum(pb, axis=-1, keepdims=True) + jnp.sum(pc, axis=-1, keepdims=True)
            o = (jnp.dot(pb.astype(BF16), vb, preferred_element_type=F32)
                 + jnp.dot(pc.astype(BF16), vc, preferred_element_type=F32))
            outs.append(o * (1.0 / l))
        o_ref[:, sl] = jnp.where(low, outs[0], outs[1]).astype(BF16)


def _na_attention(qa, ka, va, bias, *, batch, n_lat, n_ctx):
    rows = n_lat // GRID_W
    groups = rows // NA_Q_ROWS
    tq = NA_Q_ROWS * GRID_W
    per_batch = n_lat + n_ctx

    def variant(g):
        return jnp.where(g == 0, 0, jnp.where(g == groups - 1, 2, 1))

    kv_spec = _resident((per_batch, HALF_WIDTH), lambda b, g: (b, 0))
    return pl.pallas_call(
        functools.partial(_na_kernel, n_lat=n_lat, n_ctx=n_ctx, rows=rows),
        grid=(batch, groups),
        in_specs=[pl.BlockSpec((tq, HALF_WIDTH), lambda b, g: (b * groups + g, 0)),
                  kv_spec, kv_spec,
                  pl.BlockSpec((1, NA_HEADS, tq, NA_BAND_ROWS * GRID_W), lambda b, g: (variant(g), 0, 0, 0))],
        out_specs=pl.BlockSpec((tq, HALF_WIDTH), lambda b, g: (b * groups + g, 0)),
        out_shape=jax.ShapeDtypeStruct((batch * n_lat, HALF_WIDTH), BF16),
        compiler_params=_cparams(2),
        name="neighbourhood_attention",
    )(qa, ka, va, bias)


def _diff_kernel(q_ref, k_ref, vt_ref, lam_ref, g_ref, o_ref, qc_ref, s0_ref, s1_ref, m_ref, alpha_ref, acc_ref,
                 *, n_chunks, tiles_per_chunk, lam_init):
    tq = q_ref.shape[0]
    tk = tiles_per_chunk * DIFF_K_TILE
    q = q_ref[...]
    low = lax.broadcasted_iota(jnp.int32, (tq, LANES), 1) < HEAD_DIM
    zero = jnp.zeros_like(q)
    qc_ref[0:tq, :] = jnp.where(low, q, zero)
    qc_ref[tq:2 * tq, :] = jnp.where(low, zero, q)
    m_ref[...] = jnp.full(m_ref.shape, -jnp.inf, F32)
    acc_ref[...] = jnp.zeros(acc_ref.shape, F32)

    def tile_scores(c, t, s_ref):
        row0 = pl.multiple_of(c * tk + t * DIFF_K_TILE, DIFF_K_TILE)
        s = lax.dot_general(k_ref[pl.ds(row0, DIFF_K_TILE), :], qc_ref[...], NT_DIMS,
                            preferred_element_type=F32)
        s_ref[t * DIFF_K_TILE:(t + 1) * DIFF_K_TILE, :] = s
        return jnp.max(s, axis=0, keepdims=True)

    def tile_pv(c, t, s_ref, m):
        p = jnp.exp2(s_ref[t * DIFF_K_TILE:(t + 1) * DIFF_K_TILE, :] - m)
        return jnp.dot(vt_ref[c * tiles_per_chunk + t], p.astype(BF16), preferred_element_type=F32)

    def finish_max(cmax):
        m_old = m_ref[...]
        m_new = jnp.maximum(m_old, cmax)
        alpha_ref[...] = jnp.exp2(m_old - m_new)
        m_ref[...] = m_new

    def step(c, s_cur, s_nxt):
        m = m_ref[...]
        alpha = alpha_ref[...]
        pv = None
        cmax = None
        for t in range(tiles_per_chunk):
            if s_nxt is not None:
                tmax = tile_scores(c + 1, t, s_nxt)
                cmax = tmax if cmax is None else jnp.maximum(cmax, tmax)
            pv_part = tile_pv(c, t, s_cur, m)
            pv = pv_part if pv is None else pv + pv_part
        acc_ref[...] = alpha * acc_ref[...] + pv
        if s_nxt is not None:
            finish_max(cmax)

    cmax = None
    for t in range(tiles_per_chunk):
        tmax = tile_scores(0, t, s0_ref)
        cmax = tmax if cmax is None else jnp.maximum(cmax, tmax)
    finish_max(cmax)

    def pair(j, carry):
        step(2 * j, s0_ref, s1_ref)
        step(2 * j + 1, s1_ref, s0_ref)
        return carry

    lax.fori_loop(0, (n_chunks - 1) // 2, pair, 0)
    step(n_chunks - 1, s0_ref, None)
    lv = lam_ref[...]
    lam = (jnp.exp(jnp.sum(lv[0:1] * lv[1:2], axis=-1, keepdims=True))
           - jnp.exp(jnp.sum(lv[2:3] * lv[3:4], axis=-1, keepdims=True)) + lam_init)
    o = acc_ref[0:DIFF_V_DIM, :] * (1.0 / acc_ref[DIFF_V_DIM:DIFF_V_DIM + 1, :])
    d = o[:, 0:tq] - lam * o[:, tq:2 * tq]
    ms = jnp.mean(d * d, axis=0, keepdims=True)
    d = d * lax.rsqrt(ms + EPS) * g_ref[...] * (1.0 - lam_init)
    o_ref[...] = d.T.astype(BF16)


def _diff_attention(qb, kb, vbt, lam_vec, subln_g, *, batch, n_q, q_row0, k_tile0, k_tiles,
                    tiles_per_batch, q_tile, lam_init):
    q_tiles = n_q // q_tile
    q_blk0 = q_row0 // q_tile
    tiles_per_chunk = DIFF_CHUNK_TILES if k_tiles % DIFF_CHUNK_TILES == 0 else 1
    n_chunks = k_tiles // tiles_per_chunk
    assert n_chunks % 2 == 1
    tk = tiles_per_chunk * DIFF_K_TILE
    if k_tile0 == 0:
        k_spec = _resident((k_tiles * DIFF_K_TILE, LANES), lambda b, h, i: (b, h))
    else:
        k_spec = _resident((k_tiles * DIFF_K_TILE, LANES), lambda b, h, i: (b * tiles_per_batch + k_tile0, h))
    return pl.pallas_call(
        functools.partial(_diff_kernel, n_chunks=n_chunks, tiles_per_chunk=tiles_per_chunk, lam_init=lam_init),
        grid=(batch, DIFF_HEADS, q_tiles),
        in_specs=[pl.BlockSpec((q_tile, LANES), lambda b, h, i: (q_blk0 + b * q_tiles + i, h)),
                  k_spec,
                  _resident((k_tiles, DIFF_V_ROWS, DIFF_K_TILE),
                            lambda b, h, i: ((b * tiles_per_batch + k_tile0) // k_tiles, h, 0)),
                  _resident((4, HEAD_DIM), lambda b, h, i: (0, 0)),
                  _resident((DIFF_V_DIM, 1), lambda b, h, i: (0, 0))],
        out_specs=pl.BlockSpec((q_tile, LANES), lambda b, h, i: (b * q_tiles + i, h)),
        out_shape=jax.ShapeDtypeStruct((batch * n_q, HALF_WIDTH), BF16),
        scratch_shapes=[pltpu.VMEM((2 * q_tile, LANES), BF16),
                        pltpu.VMEM((tk, 2 * q_tile), F32),
                        pltpu.VMEM((tk, 2 * q_tile), F32),
                        pltpu.VMEM((1, 2 * q_tile), F32),
                        pltpu.VMEM((1, 2 * q_tile), F32),
                        pltpu.VMEM((DIFF_V_ROWS, 2 * q_tile), F32)],
        compiler_params=_cparams(3),
        name="differential_attention",
    )(qb, kb, vbt, lam_vec, subln_g.reshape(DIFF_V_DIM, 1))


def _attn_out_kernel(x_ref, a_ref, d_ref, mod_ref, w_ref, o_ref):
    y = (jnp.dot(a_ref[...], w_ref[0:HALF_WIDTH, :], preferred_element_type=F32)
         + jnp.dot(d_ref[...], w_ref[HALF_WIDTH:2 * HALF_WIDTH, :], preferred_element_type=F32))
    o_ref[...] = x_ref[...] + mod_ref[0, 5:6, :] * y


def _attn_out(x, a, dd, mods, w_out, *, x_tile_of, group_of_tile, n_tiles):
    d = x.shape[1]
    return pl.pallas_call(
        _attn_out_kernel,
        grid=(n_tiles,),
        in_specs=[pl.BlockSpec((ROW_TILE, d), lambda i: (x_tile_of(i), 0)),
                  pl.BlockSpec((ROW_TILE, HALF_WIDTH), lambda i: (i, 0)),
                  pl.BlockSpec((ROW_TILE, HALF_WIDTH), lambda i: (i, 0)),
                  pl.BlockSpec((1, N_MOD, d), lambda i: (group_of_tile(i), 0, 0)),
                  _resident((2 * HALF_WIDTH, d), lambda i: (0, 0))],
        out_specs=pl.BlockSpec((ROW_TILE, d), lambda i: (i, 0)),
        out_shape=jax.ShapeDtypeStruct((n_tiles * ROW_TILE, d), F32),
        compiler_params=_cparams(1),
        name="attn_out_proj",
    )(x, a, dd, mods, w_out)


def _dense_kernel(q_ref, k_ref, v_ref, o_ref):
    tq = q_ref.shape[0]
    low = lax.broadcasted_iota(jnp.int32, (tq, LANES), 1) < HEAD_DIM
    for p in range(NA_HEADS // 2):
        sl = slice(p * LANES, (p + 1) * LANES)
        q, k, v = q_ref[:, sl], k_ref[:, sl], v_ref[:, sl]
        outs = []
        for e in range(2):
            qm = jnp.where(low if e == 0 else jnp.logical_not(low), q, jnp.zeros_like(q))
            s = lax.dot_general(qm, k, NT_DIMS, preferred_element_type=F32)
            pr = jnp.exp(s - jnp.max(s, axis=-1, keepdims=True))
            o = jnp.dot(pr.astype(BF16), v, preferred_element_type=F32)
            outs.append(o * (1.0 / jnp.sum(pr, axis=-1, keepdims=True)))
        o_ref[:, sl] = jnp.where(low, outs[0], outs[1]).astype(BF16)


def _dense_attention(qa, ka, va, *, batch, q_row0, k_tile0, tiles_per_batch):
    q_blk0 = q_row0 // ROW_TILE
    kv_spec = pl.BlockSpec((ROW_TILE, HALF_WIDTH), lambda b: (b * tiles_per_batch + k_tile0, 0))
    return pl.pallas_call(
        _dense_kernel,
        grid=(batch,),
        in_specs=[pl.BlockSpec((ROW_TILE, HALF_WIDTH), lambda b: (q_blk0 + b, 0)), kv_spec, kv_spec],
        out_specs=pl.BlockSpec((ROW_TILE, HALF_WIDTH), lambda b: (b, 0)),
        out_shape=jax.ShapeDtypeStruct((batch * ROW_TILE, HALF_WIDTH), BF16),
        compiler_params=_cparams(1),
        name="context_attention",
    )(qa, ka, va)


def _conv_kernel(x_ref, xp_ref, xs_ref, mod_ref, g_ref, w_in_ref, cw_ref, w_out_ref, o_ref, *, tiles_per_seq):
    i = pl.program_id(0)
    tile, d = x_ref.shape
    halo = xp_ref.shape[0]
    x = x_ref[...]
    x_ext = jnp.concatenate([xp_ref[...], x, xs_ref[...]], axis=0)
    xn = _modulate(x_ext, g_ref[...], mod_ref[0, 3:4, :], mod_ref[0, 4:5, :]).astype(BF16)
    cg = jnp.dot(xn, w_in_ref[:, d:2 * d], preferred_element_type=F32)
    hh = jnp.dot(xn, w_in_ref[:, 2 * d:3 * d], preferred_element_type=F32)
    u = cg * hh
    n_ext = tile + 2 * halo
    ridx = lax.broadcasted_iota(jnp.int32, (n_ext, 1), 0)
    first = (i % tiles_per_seq) == 0
    last = (i % tiles_per_seq) == tiles_per_seq - 1
    u = jnp.where(jnp.logical_or(jnp.logical_and(first, ridx < halo),
                                 jnp.logical_and(last, ridx >= halo + tile)), 0.0, u)
    cw = cw_ref[...]
    u_prev = pltpu.roll(u, 1, 0)[halo:halo + tile]
    u_next = pltpu.roll(u, n_ext - 1, 0)[halo:halo + tile]
    v = u_prev * cw[0:1] + u[halo:halo + tile] * cw[1:2] + u_next * cw[2:3]
    bg = jnp.dot(xn[halo:halo + tile], w_in_ref[:, 0:d], preferred_element_type=F32)
    y = jnp.dot((bg * v).astype(BF16), w_out_ref[...], preferred_element_type=F32)
    o_ref[...] = x + mod_ref[0, 5:6, :] * y


def _conv_mixer(x, mods, group_of_tile, g, w_in, conv_w, w_out, *, tile, seq):
    rows, d = x.shape
    halo = 8
    tiles_per_seq = seq // tile
    hb = tile // halo
    n_hblocks = rows // halo
    return pl.pallas_call(
        functools.partial(_conv_kernel, tiles_per_seq=tiles_per_seq),
        grid=(rows // tile,),
        in_specs=[pl.BlockSpec((tile, d), lambda i: (i, 0)),
                  pl.BlockSpec((halo, d), lambda i: (jnp.maximum(i * hb - 1, 0), 0)),
                  pl.BlockSpec((halo, d), lambda i: (jnp.minimum((i + 1) * hb, n_hblocks - 1), 0)),
                  pl.BlockSpec((1, N_MOD, d), lambda i: (group_of_tile(i), 0, 0)),
                  _resident((1, d), lambda i: (0, 0)),
                  _resident((d, 3 * d), lambda i: (0, 0)),
                  _resident((CONV_WIDTH, d), lambda i: (0, 0)),
                  _resident((d, d), lambda i: (0, 0))],
        out_specs=pl.BlockSpec((tile, d), lambda i: (i, 0)),
        out_shape=jax.ShapeDtypeStruct((rows, d), F32),
        compiler_params=_cparams(1),
        name="conv_mixer",
    )(x, x, x, mods, g.reshape(1, d), w_in, conv_w, w_out)


def kernel(x, c, ctx, c_ctx, mod_w, mod_b, norm_g, ffn_w_gate, ffn_w_up, ffn_w_down, attn_w_in, attn_w_out,
           na_rpb, diff_lambda, diff_subln_g, conv_w_in, conv_w_out, conv_w, final_g):
    batch, n_lat, d = x.shape
    n_ctx = ctx.shape[1]
    depth = mod_w.shape[0]
    assert n_ctx == ROW_TILE and n_lat % LAT_TILE == 0 and depth == 2 and batch + 1 <= 8
    tiles_per_batch = (n_lat + n_ctx) // ROW_TILE
    lat_tiles = n_lat // ROW_TILE
    bf = lambda a: a.astype(BF16)

    cvec = jnp.concatenate([c, c_ctx[None, :], jnp.zeros((8 - batch - 1, d), F32)], axis=0)
    mods_all = _mod_vectors(cvec, mod_w, mod_b).reshape(depth, 8, N_MOD, d)

    def comb_group(i):
        return jnp.where(i % tiles_per_batch == lat_tiles, batch, i // tiles_per_batch)

    mods = mods_all[0]
    xc = jnp.concatenate([x, ctx], axis=1).reshape(batch * (n_lat + n_ctx), d)
    xc = _ffn(xc, mods, comb_group, norm_g[0, 0], bf(ffn_w_gate[0, 0]), bf(ffn_w_up[0, 0]),
              bf(ffn_w_down[0, 0]), tile=ROW_TILE, mod_base=0)
    qa, ka, va, qb, kb, vbt = _attn_in(xc, mods, norm_g[0, 1], bf(attn_w_in[0]),
                                       _rope_tables(n_lat, n_ctx), batch=batch, tiles_per_batch=tiles_per_batch)
    lam_init = 0.8 - 0.6 * math.exp(-0.3 * 0)
    a_lat = _na_attention(qa, ka, va, _na_bias_tables(na_rpb[0], n_lat // GRID_W),
                          batch=batch, n_lat=n_lat, n_ctx=n_ctx)
    d_lat = _diff_attention(qb, kb, vbt, diff_lambda[0], diff_subln_g[0], batch=batch, n_q=n_lat, q_row0=0,
                            k_tile0=0, k_tiles=tiles_per_batch, tiles_per_batch=tiles_per_batch,
                            q_tile=DIFF_Q_TILE, lam_init=lam_init)
    w_out = bf(attn_w_out[0])
    x_lat = _attn_out(xc, a_lat, d_lat, mods, w_out, n_tiles=batch * lat_tiles,
                      x_tile_of=lambda i: (i // lat_tiles) * tiles_per_batch + i % lat_tiles,
                      group_of_tile=lambda i: i // lat_tiles)
    lat_group = lambda i: i // (n_lat // LAT_TILE)
    x_lat = _ffn(x_lat, mods, lat_group, norm_g[0, 2], bf(ffn_w_gate[0, 1]), bf(ffn_w_up[0, 1]),
                 bf(ffn_w_down[0, 1]), tile=LAT_TILE, mod_base=6)

    a_ctx = _dense_attention(qa, ka, va, batch=batch, q_row0=batch * n_lat, k_tile0=lat_tiles,
                             tiles_per_batch=tiles_per_batch)
    d_ctx = _diff_attention(qb, kb, vbt, diff_lambda[0], diff_subln_g[0], batch=batch, n_q=n_ctx,
                            q_row0=batch * n_lat, k_tile0=lat_tiles, k_tiles=1,
                            tiles_per_batch=tiles_per_batch, q_tile=ROW_TILE, lam_init=lam_init)
    x_ctx = _attn_out(xc, a_ctx, d_ctx, mods, w_out, n_tiles=batch,
                      x_tile_of=lambda i: i * tiles_per_batch + lat_tiles, group_of_tile=lambda i: batch)
    x_ctx = _ffn(x_ctx, mods, lambda i: batch, norm_g[0, 2], bf(ffn_w_gate[0, 1]), bf(ffn_w_up[0, 1]),
                 bf(ffn_w_down[0, 1]), tile=ROW_TILE, mod_base=6)
    del x_ctx

    mods = mods_all[1]
    x_lat = _ffn(x_lat, mods, lat_group, norm_g[1, 0], bf(ffn_w_gate[1, 0]), bf(ffn_w_up[1, 0]),
                 bf(ffn_w_down[1, 0]), tile=LAT_TILE, mod_base=0)
    x_lat = _conv_mixer(x_lat, mods, lat_group, norm_g[1, 1], bf(conv_w_in[0]), conv_w[0], bf(conv_w_out[0]),
                        tile=LAT_TILE, seq=n_lat)
    x_lat = _ffn(x_lat, mods, lat_group, norm_g[1, 2], bf(ffn_w_gate[1, 1]), bf(ffn_w_up[1, 1]),
                 bf(ffn_w_down[1, 1]), tile=LAT_TILE, mod_base=6, final_g=final_g)
    return x_lat.reshape(batch, n_lat, d)
```

```python
import functools
import math

import jax
import jax.numpy as jnp
import numpy as np
from jax import lax
from jax.experimental import pallas as pl
from jax.experimental.pallas import tpu as pltpu

F32 = jnp.float32
BF16 = jnp.bfloat16

D_MODEL = 1024
GRID_W = 64
HEAD_DIM = 64
NA_HEADS = 8
NA_WIN_ROWS = 8
NA_WIN_COLS = 16
DIFF_HEADS = 4
DIFF_V_DIM = 128
DIFF_V_ROWS = 144
LOG2_E = math.log2(math.e)
HALF_WIDTH = 512
ATTN_IN_WIDTH = 6 * HALF_WIDTH
CONV_WIDTH = 3
D_FF = 2816
ROPE_THETA = 10000.0
N_MOD = 9
EPS = 1e-6
NEG_INF = -1e30

LANES = 128
ROW_TILE = 256
LAT_TILE = 512
NA_Q_ROWS = 4
NA_BAND_ROWS = 12
DIFF_Q_TILE = 512
DIFF_K_TILE = 256
DIFF_CHUNK_TILES = 3
FF_CHUNKS = ((0, 1024), (1024, 1024), (2048, 768))
VMEM_LIMIT = 56 * 1024 * 1024

NT_DIMS = (((1,), (1,)), ((), ()))


def _cparams(n_axes):
    return pltpu.CompilerParams(dimension_semantics=("arbitrary",) * n_axes,
                                vmem_limit_bytes=VMEM_LIMIT)


def _resident(shape, index_map):
    return pl.BlockSpec(shape, index_map, pipeline_mode=pl.Buffered(1))


def _modulate(x, g, shift, scale):
    ms = jnp.mean(x * x, axis=-1, keepdims=True)
    return x * lax.rsqrt(ms + EPS) * g * (1.0 + scale) + shift


def _silu(x):
    return x * (1.0 / (1.0 + jnp.exp(-x)))


def _mod_kernel(c_ref, w_ref, b_ref, o_ref):
    o_ref[0] = jnp.dot(_silu(c_ref[...]), w_ref[0], preferred_element_type=F32) + b_ref[0]


def _mod_vectors(cvec, mod_w, mod_b):
    depth, d, n = mod_w.shape
    bn = 1536
    return pl.pallas_call(
        _mod_kernel,
        grid=(depth, n // bn),
        in_specs=[pl.BlockSpec((8, d), lambda i, j: (0, 0)),
                  pl.BlockSpec((1, d, bn), lambda i, j: (i, 0, j)),
                  pl.BlockSpec((1, 1, bn), lambda i, j: (i, 0, j))],
        out_specs=pl.BlockSpec((1, 8, bn), lambda i, j: (i, 0, j)),
        out_shape=jax.ShapeDtypeStruct((depth, 8, n), F32),
        compiler_params=_cparams(2),
        name="adaln_vectors",
    )(cvec, mod_w, mod_b.reshape(depth, 1, n))


def _ffn_kernel(*refs, mod_base, ctx_period, attn, final):
    refs = list(refs)
    o_ref = refs.pop()
    x_ref = refs.pop(0)
    x = x_ref[...]
    if ctx_period is not None:
        ctx_ref = refs.pop(0)
        is_ctx = pl.program_id(0) % ctx_period == ctx_period - 1
        x = jnp.where(is_ctx, ctx_ref[...], x)
    if attn:
        a_ref, d_ref, wo_ref = refs.pop(0), refs.pop(0), refs.pop(0)
    mod_ref, g_ref, wg_ref, wu_ref, wd_ref = refs[:5]
    if attn:
        y = (jnp.dot(a_ref[...], wo_ref[0:HALF_WIDTH, :], preferred_element_type=F32)
             + jnp.dot(d_ref[...], wo_ref[HALF_WIDTH:2 * HALF_WIDTH, :], preferred_element_type=F32))
        x = x + mod_ref[0, 5:6, :] * y
    shift = mod_ref[0, mod_base:mod_base + 1, :]
    scale = mod_ref[0, mod_base + 1:mod_base + 2, :]
    gate = mod_ref[0, mod_base + 2:mod_base + 3, :]
    xn = _modulate(x, g_ref[...], shift, scale).astype(BF16)
    y = None
    for lo, width in FF_CHUNKS:
        hg = jnp.dot(xn, wg_ref[:, lo:lo + width], preferred_element_type=F32)
        hu = jnp.dot(xn, wu_ref[:, lo:lo + width], preferred_element_type=F32)
        a = (_silu(hg) * hu).astype(BF16)
        part = jnp.dot(a, wd_ref[lo:lo + width, :], preferred_element_type=F32)
        y = part if y is None else y + part
    out = x + (0.5 * gate) * y
    if final:
        fg = refs[5][...]
        ms = jnp.mean(out * out, axis=-1, keepdims=True)
        out = out * lax.rsqrt(ms + EPS) * fg
    o_ref[...] = out


def _ffn(x, mods, g, weights, layer, which, *, tile, n_tiles, mod_base, group_of_tile, x_tile_of=None,
         ctx=None, ctx_period=None, attn=None, final_g=None):
    d = x.shape[1]
    wg, wu, wd = weights
    f = wg.shape[-1]
    x_tile_of = x_tile_of or (lambda i: i)
    in_specs = [pl.BlockSpec((tile, d), lambda i: (x_tile_of(i), 0))]
    args = [x]
    if ctx is not None:
        in_specs.append(pl.BlockSpec((tile, d), lambda i: (i // ctx_period, 0)))
        args.append(ctx)
    if attn is not None:
        a, dd, w_out, ad_tile_of = attn
        in_specs += [pl.BlockSpec((tile, HALF_WIDTH), lambda i: (ad_tile_of(i), 0)),
                     pl.BlockSpec((tile, HALF_WIDTH), lambda i: (ad_tile_of(i), 0)),
                     _resident((2 * HALF_WIDTH, d), lambda i: (0, 0))]
        args += [a, dd, w_out]
    in_specs += [pl.BlockSpec((1, N_MOD, d), lambda i: (group_of_tile(i), 0, 0)),
                 _resident((1, d), lambda i: (0, 0)),
                 _resident((None, None, d, f), lambda i: (layer, which, 0, 0)),
                 _resident((None, None, d, f), lambda i: (layer, which, 0, 0)),
                 _resident((None, None, f, d), lambda i: (layer, which, 0, 0))]
    args += [mods, g.reshape(1, d), wg, wu, wd]
    if final_g is not None:
        in_specs.append(_resident((1, d), lambda i: (0, 0)))
        args.append(final_g.reshape(1, d))
    return pl.pallas_call(
        functools.partial(_ffn_kernel, mod_base=mod_base, ctx_period=ctx_period if ctx is not None else None,
                          attn=attn is not None, final=final_g is not None),
        grid=(n_tiles,),
        in_specs=in_specs,
        out_specs=pl.BlockSpec((tile, d), lambda i: (i, 0)),
        out_shape=jax.ShapeDtypeStruct((n_tiles * tile, d), F32),
        compiler_params=_cparams(1),
        name="macaron_ffn",
    )(*args)


def _rope(z, cos, sin_next, sin_prev):
    return z * cos + pltpu.roll(z, LANES - 1, 1) * sin_next + pltpu.roll(z, 1, 1) * sin_prev


def _attn_in_kernel(x_ref, mod_ref, g_ref, w_ref, cos_ref, sn_ref, sp_ref,
                    qa_ref, ka_ref, va_ref, qb_ref, kb_ref, vbt_ref):
    x = x_ref[...]
    xn = _modulate(x, g_ref[...], mod_ref[0, 3:4, :], mod_ref[0, 4:5, :]).astype(BF16)
    w = HALF_WIDTH
    scale = HEAD_DIM ** -0.5
    qa_ref[...] = (jnp.dot(xn, w_ref[:, 0:w], preferred_element_type=F32) * scale).astype(BF16)
    ka_ref[...] = jnp.dot(xn, w_ref[:, w:2 * w], preferred_element_type=F32).astype(BF16)
    va_ref[...] = jnp.dot(xn, w_ref[:, 2 * w:3 * w], preferred_element_type=F32).astype(BF16)
    cos, sn, sp = cos_ref[...], sn_ref[...], sp_ref[...]
    qb = jnp.dot(xn, w_ref[:, 3 * w:4 * w], preferred_element_type=F32)
    kb = jnp.dot(xn, w_ref[:, 4 * w:5 * w], preferred_element_type=F32)
    for h in range(DIFF_HEADS):
        sl = slice(h * LANES, (h + 1) * LANES)
        qb_ref[:, sl] = (_rope(qb[:, sl], cos, sn, sp) * (scale * LOG2_E)).astype(BF16)
        kb_ref[:, sl] = _rope(kb[:, sl], cos, sn, sp).astype(BF16)
    vbt = jnp.dot(xn, w_ref[:, 5 * w:6 * w], preferred_element_type=F32).T.astype(BF16)
    pad_rows = DIFF_V_ROWS - DIFF_V_DIM
    ones_row = (lax.broadcasted_iota(jnp.int32, (pad_rows, vbt.shape[1]), 0) == 0).astype(BF16)
    for h in range(DIFF_HEADS):
        vbt_ref[0, h * DIFF_V_ROWS:h * DIFF_V_ROWS + DIFF_V_DIM, :] = vbt[h * DIFF_V_DIM:(h + 1) * DIFF_V_DIM]
        vbt_ref[0, h * DIFF_V_ROWS + DIFF_V_DIM:(h + 1) * DIFF_V_ROWS, :] = ones_row


def _rope_tables(n_lat, n_ctx):
    t = jnp.arange(n_lat)
    row = (t // GRID_W).astype(F32)
    col = (t % GRID_W).astype(F32)
    n_freq = HEAD_DIM // 4
    inv = ROPE_THETA ** (-jnp.arange(n_freq, dtype=F32) / n_freq)
    ang = jnp.concatenate([row[:, None] * inv, col[:, None] * inv], axis=-1)
    cos = jnp.repeat(jnp.cos(ang), 2, axis=-1)
    sin = jnp.repeat(jnp.sin(ang), 2, axis=-1)
    even = (jnp.arange(HEAD_DIM) % 2 == 0)[None, :]
    sin_next = jnp.where(even, -sin, 0.0)
    sin_prev = jnp.where(even, 0.0, sin)
    pad = lambda a, v: jnp.concatenate([a, jnp.full((n_ctx, HEAD_DIM), v, F32)], axis=0)
    two = lambda a: jnp.concatenate([a, a], axis=-1)
    return two(pad(cos, 1.0)), two(pad(sin_next, 0.0)), two(pad(sin_prev, 0.0))


def _attn_in(xc, mods, g, w_in, tables, *, batch, tiles_per_batch):
    rows, d = xc.shape
    lat_tiles = tiles_per_batch - 1
    n_tiles = rows // ROW_TILE

    def group(i):
        return jnp.where(i % tiles_per_batch == lat_tiles, batch, i // tiles_per_batch)

    def lat_first(i):
        b, j = i // tiles_per_batch, i % tiles_per_batch
        return jnp.where(j == lat_tiles, batch * lat_tiles + b, b * lat_tiles + j)

    tab_spec = pl.BlockSpec((ROW_TILE, LANES), lambda i: (i % tiles_per_batch, 0))
    half = jax.ShapeDtypeStruct((rows, HALF_WIDTH), BF16)
    q_spec = pl.BlockSpec((ROW_TILE, HALF_WIDTH), lambda i: (lat_first(i), 0))
    kv_spec = pl.BlockSpec((ROW_TILE, HALF_WIDTH), lambda i: (i, 0))
    return pl.pallas_call(
        _attn_in_kernel,
        grid=(n_tiles,),
        in_specs=[pl.BlockSpec((ROW_TILE, d), lambda i: (i, 0)),
                  pl.BlockSpec((1, N_MOD, d), lambda i: (group(i), 0, 0)),
                  _resident((1, d), lambda i: (0, 0)),
                  _resident((d, ATTN_IN_WIDTH), lambda i: (0, 0)),
                  tab_spec, tab_spec, tab_spec],
        out_specs=[q_spec, kv_spec, kv_spec, q_spec, kv_spec,
                   pl.BlockSpec((1, DIFF_HEADS * DIFF_V_ROWS, ROW_TILE), lambda i: (i, 0, 0))],
        out_shape=[half, half, half, half, half,
                   jax.ShapeDtypeStruct((n_tiles, DIFF_HEADS * DIFF_V_ROWS, ROW_TILE), BF16)],
        compiler_params=_cparams(1),
        name="attn_in_proj",
    )(xc, mods, g.reshape(1, d), w_in, *tables)


def _na_bias_tables(rpb, rows):
    heads = rpb.shape[0]
    kr_win = min(NA_WIN_ROWS, rows)
    qc = np.arange(GRID_W)
    col_start = np.clip(qc - NA_WIN_COLS // 2, 0, GRID_W - NA_WIN_COLS)
    col_ok = (qc[None, :] >= col_start[:, None]) & (qc[None, :] < col_start[:, None] + NA_WIN_COLS)
    dc = np.clip(qc[None, :] - qc[:, None] + NA_WIN_COLS - 1, 0, 2 * NA_WIN_COLS - 2)
    rpb = rpb.astype(F32)
    t = jnp.full((heads, 2 * NA_WIN_ROWS - 1, GRID_W, GRID_W), NEG_INF, F32)
    for cidx in range(2 * NA_WIN_COLS - 1):
        t = jnp.where((col_ok & (dc == cidx))[None, None], rpb[:, :, cidx][:, :, None, None], t)
    neg = jnp.full((heads, GRID_W, GRID_W), NEG_INF, F32)
    out = []
    for r0, ws in ((0, 0), (NA_Q_ROWS, 0), (rows - NA_Q_ROWS, rows - NA_BAND_ROWS)):
        block_rows = []
        for i in range(NA_Q_ROWS):
            qr = r0 + i
            start = min(max(qr - kr_win // 2, 0), rows - kr_win)
            blocks = [t[:, kr - qr + NA_WIN_ROWS - 1] if start <= kr < start + kr_win else neg
                      for kr in range(ws, ws + NA_BAND_ROWS)]
            block_rows.append(jnp.concatenate(blocks, axis=-1))
        out.append(jnp.concatenate(block_rows, axis=1))
    return jnp.stack(out)


def _na_kernel(q_ref, k_ref, v_ref, bias_ref, o_ref, *, n_lat, n_ctx, rows):
    g = pl.program_id(1)
    ws = jnp.clip(NA_Q_ROWS * g - NA_WIN_ROWS // 2, 0, rows - NA_BAND_ROWS)
    start = pl.multiple_of(ws * GRID_W, GRID_W)
    band = NA_BAND_ROWS * GRID_W
    tq = NA_Q_ROWS * GRID_W
    low = lax.broadcasted_iota(jnp.int32, (tq, LANES), 1) < HEAD_DIM
    for p in range(NA_HEADS // 2):
        sl = slice(p * LANES, (p + 1) * LANES)
        q = q_ref[:, sl]
        kb = k_ref[pl.ds(start, band), sl]
        vb = v_ref[pl.ds(start, band), sl]
        kc = k_ref[n_lat:n_lat + n_ctx, sl]
        vc = v_ref[n_lat:n_lat + n_ctx, sl]
        outs = []
        for e in range(2):
            qm = jnp.where(low if e == 0 else jnp.logical_not(low), q, jnp.zeros_like(q))
            sb = lax.dot_general(qm, kb, NT_DIMS, preferred_element_type=F32) + bias_ref[0, 2 * p + e]
            sc = lax.dot_general(qm, kc, NT_DIMS, preferred_element_type=F32)
            m = jnp.maximum(jnp.max(sb, axis=-1, keepdims=True), jnp.max(sc, axis=-1, keepdims=True))
            pb = jnp.exp(sb - m)
            pc = jnp.exp(sc - m)
            l = jnp.sum(pb, axis=-1, keepdims=True) + jnp.sum(pc, axis=-1, keepdims=True)
            o = (jnp.dot(pb.astype(BF16), vb, preferred_element_type=F32)
                 + jnp.dot(pc.astype(BF16), vc, preferred_element_type=F32))
            outs.append(o * (1.0 / l))
        o_ref[:, sl] = jnp.where(low, outs[0], outs[1]).astype(BF16)


def _na_attention(qa, ka, va, bias, *, batch, n_lat, n_ctx):
    rows = n_lat // GRID_W
    groups = rows // NA_Q_ROWS
    tq = NA_Q_ROWS * GRID_W
    per_batch = n_lat + n_ctx

    def variant(g):
        return jnp.where(g == 0, 0, jnp.where(g == groups - 1, 2, 1))

    kv_spec = _resident((per_batch, HALF_WIDTH), lambda b, g: (b, 0))
    return pl.pallas_call(
        functools.partial(_na_kernel, n_lat=n_lat, n_ctx=n_ctx, rows=rows),
        grid=(batch, groups),
        in_specs=[pl.BlockSpec((tq, HALF_WIDTH), lambda b, g: (b * groups + g, 0)),
                  kv_spec, kv_spec,
                  pl.BlockSpec((1, NA_HEADS, tq, NA_BAND_ROWS * GRID_W), lambda b, g: (variant(g), 0, 0, 0))],
        out_specs=pl.BlockSpec((tq, HALF_WIDTH), lambda b, g: (b * groups + g, 0)),
        out_shape=jax.ShapeDtypeStruct((batch * n_lat, HALF_WIDTH), BF16),
        compiler_params=_cparams(2),
        name="neighbourhood_attention",
    )(qa, ka, va, bias)


def _diff_kernel(q_ref, k_ref, vt_ref, lam_ref, g_ref, o_ref, qc_ref, s_ref, m_ref, alpha_ref, acc_ref,
                 *, n_chunks, tiles_per_chunk, lam_init):
    tq = q_ref.shape[0]
    tk = tiles_per_chunk * DIFF_K_TILE
    q = q_ref[...]
    low = lax.broadcasted_iota(jnp.int32, (tq, LANES), 1) < HEAD_DIM
    zero = jnp.zeros_like(q)
    qc_ref[0:tq, :] = jnp.where(low, q, zero)
    qc_ref[tq:2 * tq, :] = jnp.where(low, zero, q)
    m_ref[...] = jnp.full(m_ref.shape, -jnp.inf, F32)
    acc_ref[...] = jnp.zeros(acc_ref.shape, F32)

    def tile_scores(c, t):
        row0 = pl.multiple_of(c * tk + t * DIFF_K_TILE, DIFF_K_TILE)
        s = lax.dot_general(k_ref[pl.ds(row0, DIFF_K_TILE), :], qc_ref[...], NT_DIMS,
                            preferred_element_type=F32)
        s_ref[t * DIFF_K_TILE:(t + 1) * DIFF_K_TILE, :] = s
        return jnp.max(s, axis=0, keepdims=True)

    def finish_max(cmax):
        m_old = m_ref[...]
        m_new = jnp.maximum(m_old, cmax)
        alpha_ref[...] = jnp.exp2(m_old - m_new)
        m_ref[...] = m_new

    def step(c, has_next):
        m = m_ref[...]
        alpha = alpha_ref[...]
        cmax = None
        for t in range(tiles_per_chunk):
            p = jnp.exp2(s_ref[t * DIFF_K_TILE:(t + 1) * DIFF_K_TILE, :] - m).astype(BF16)
            if has_next:
                tmax = tile_scores(c + 1, t)
                cmax = tmax if cmax is None else jnp.maximum(cmax, tmax)
            pv = jnp.dot(vt_ref[c * tiles_per_chunk + t], p, preferred_element_type=F32)
            acc_ref[...] = (alpha * acc_ref[...] if t == 0 else acc_ref[...]) + pv
        if has_next:
            finish_max(cmax)

    cmax = None
    for t in range(tiles_per_chunk):
        tmax = tile_scores(0, t)
        cmax = tmax if cmax is None else jnp.maximum(cmax, tmax)
    finish_max(cmax)

    def body(c, carry):
        step(c, True)
        return carry

    lax.fori_loop(0, n_chunks - 1, body, 0)
    step(n_chunks - 1, False)
    lv = lam_ref[...]
    lam = (jnp.exp(jnp.sum(lv[0:1] * lv[1:2], axis=-1, keepdims=True))
           - jnp.exp(jnp.sum(lv[2:3] * lv[3:4], axis=-1, keepdims=True)) + lam_init)
    o = acc_ref[0:DIFF_V_DIM, :] * (1.0 / acc_ref[DIFF_V_DIM:DIFF_V_DIM + 1, :])
    d = o[:, 0:tq] - lam * o[:, tq:2 * tq]
    ms = jnp.mean(d * d, axis=0, keepdims=True)
    d = d * lax.rsqrt(ms + EPS) * g_ref[...] * (1.0 - lam_init)
    o_ref[...] = d.T.astype(BF16)


def _diff_attention(qb, kb, vbt, lam_vec, subln_g, *, batch, n_q, q_row0, k_tile0, k_tiles,
                    tiles_per_batch, q_tile, lam_init):
    q_tiles = n_q // q_tile
    q_blk0 = q_row0 // q_tile
    tiles_per_chunk = DIFF_CHUNK_TILES if k_tiles % DIFF_CHUNK_TILES == 0 else 1
    n_chunks = k_tiles // tiles_per_chunk
    tk = tiles_per_chunk * DIFF_K_TILE
    if k_tile0 == 0:
        k_spec = _resident((k_tiles * DIFF_K_TILE, LANES), lambda b, h, i: (b, h))
    else:
        k_spec = _resident((k_tiles * DIFF_K_TILE, LANES), lambda b, h, i: (b * tiles_per_batch + k_tile0, h))
    return pl.pallas_call(
        functools.partial(_diff_kernel, n_chunks=n_chunks, tiles_per_chunk=tiles_per_chunk, lam_init=lam_init),
        grid=(batch, DIFF_HEADS, q_tiles),
        in_specs=[pl.BlockSpec((q_tile, LANES), lambda b, h, i: (q_blk0 + b * q_tiles + i, h)),
                  k_spec,
                  _resident((k_tiles, DIFF_V_ROWS, DIFF_K_TILE),
                            lambda b, h, i: ((b * tiles_per_batch + k_tile0) // k_tiles, h, 0)),
                  _resident((4, HEAD_DIM), lambda b, h, i: (0, 0)),
                  _resident((DIFF_V_DIM, 1), lambda b, h, i: (0, 0))],
        out_specs=pl.BlockSpec((q_tile, LANES), lambda b, h, i: (b * q_tiles + i, h)),
        out_shape=jax.ShapeDtypeStruct((batch * n_q, HALF_WIDTH), BF16),
        scratch_shapes=[pltpu.VMEM((2 * q_tile, LANES), BF16),
                        pltpu.VMEM((tk, 2 * q_tile), F32),
                        pltpu.VMEM((1, 2 * q_tile), F32),
                        pltpu.VMEM((1, 2 * q_tile), F32),
                        pltpu.VMEM((DIFF_V_ROWS, 2 * q_tile), F32)],
        compiler_params=_cparams(3),
        name="differential_attention",
    )(qb, kb, vbt, lam_vec, subln_g.reshape(DIFF_V_DIM, 1))


def _dense_kernel(q_ref, k_ref, v_ref, o_ref):
    tq = q_ref.shape[0]
    low = lax.broadcasted_iota(jnp.int32, (tq, LANES), 1) < HEAD_DIM
    for p in range(NA_HEADS // 2):
        sl = slice(p * LANES, (p + 1) * LANES)
        q, k, v = q_ref[:, sl], k_ref[:, sl], v_ref[:, sl]
        outs = []
        for e in range(2):
            qm = jnp.where(low if e == 0 else jnp.logical_not(low), q, jnp.zeros_like(q))
            s = lax.dot_general(qm, k, NT_DIMS, preferred_element_type=F32)
            pr = jnp.exp(s - jnp.max(s, axis=-1, keepdims=True))
            o = jnp.dot(pr.astype(BF16), v, preferred_element_type=F32)
            outs.append(o * (1.0 / jnp.sum(pr, axis=-1, keepdims=True)))
        o_ref[:, sl] = jnp.where(low, outs[0], outs[1]).astype(BF16)


def _dense_attention(qa, ka, va, *, batch, q_row0, k_tile0, tiles_per_batch):
    q_blk0 = q_row0 // ROW_TILE
    kv_spec = pl.BlockSpec((ROW_TILE, HALF_WIDTH), lambda b: (b * tiles_per_batch + k_tile0, 0))
    return pl.pallas_call(
        _dense_kernel,
        grid=(batch,),
        in_specs=[pl.BlockSpec((ROW_TILE, HALF_WIDTH), lambda b: (q_blk0 + b, 0)), kv_spec, kv_spec],
        out_specs=pl.BlockSpec((ROW_TILE, HALF_WIDTH), lambda b: (b, 0)),
        out_shape=jax.ShapeDtypeStruct((batch * ROW_TILE, HALF_WIDTH), BF16),
        compiler_params=_cparams(1),
        name="context_attention",
    )(qa, ka, va)


def _conv_kernel(x_ref, xp_ref, xs_ref, mod_ref, g_ref, w_in_ref, cw_ref, w_out_ref, o_ref, *, tiles_per_seq):
    i = pl.program_id(0)
    tile, d = x_ref.shape
    halo = xp_ref.shape[0]
    x = x_ref[...]
    x_ext = jnp.concatenate([xp_ref[...], x, xs_ref[...]], axis=0)
    xn = _modulate(x_ext, g_ref[...], mod_ref[0, 3:4, :], mod_ref[0, 4:5, :]).astype(BF16)
    cg = jnp.dot(xn, w_in_ref[:, d:2 * d], preferred_element_type=F32)
    hh = jnp.dot(xn, w_in_ref[:, 2 * d:3 * d], preferred_element_type=F32)
    u = cg * hh
    n_ext = tile + 2 * halo
    ridx = lax.broadcasted_iota(jnp.int32, (n_ext, 1), 0)
    first = (i % tiles_per_seq) == 0
    last = (i % tiles_per_seq) == tiles_per_seq - 1
    u = jnp.where(jnp.logical_or(jnp.logical_and(first, ridx < halo),
                                 jnp.logical_and(last, ridx >= halo + tile)), 0.0, u)
    cw = cw_ref[...]
    u_prev = pltpu.roll(u, 1, 0)[halo:halo + tile]
    u_next = pltpu.roll(u, n_ext - 1, 0)[halo:halo + tile]
    v = u_prev * cw[0:1] + u[halo:halo + tile] * cw[1:2] + u_next * cw[2:3]
    bg = jnp.dot(xn[halo:halo + tile], w_in_ref[:, 0:d], preferred_element_type=F32)
    y = jnp.dot((bg * v).astype(BF16), w_out_ref[...], preferred_element_type=F32)
    o_ref[...] = x + mod_ref[0, 5:6, :] * y


def _conv_mixer(x, mods, group_of_tile, g, w_in, conv_w, w_out, *, tile, seq):
    rows, d = x.shape
    halo = 8
    tiles_per_seq = seq // tile
    hb = tile // halo
    n_hblocks = rows // halo
    return pl.pallas_call(
        functools.partial(_conv_kernel, tiles_per_seq=tiles_per_seq),
        grid=(rows // tile,),
        in_specs=[pl.BlockSpec((tile, d), lambda i: (i, 0)),
                  pl.BlockSpec((halo, d), lambda i: (jnp.maximum(i * hb - 1, 0), 0)),
                  pl.BlockSpec((halo, d), lambda i: (jnp.minimum((i + 1) * hb, n_hblocks - 1), 0)),
                  pl.BlockSpec((1, N_MOD, d), lambda i: (group_of_tile(i), 0, 0)),
                  _resident((1, d), lambda i: (0, 0)),
                  _resident((d, 3 * d), lambda i: (0, 0)),
                  _resident((CONV_WIDTH, d), lambda i: (0, 0)),
                  _resident((d, d), lambda i: (0, 0))],
        out_specs=pl.BlockSpec((tile, d), lambda i: (i, 0)),
        out_shape=jax.ShapeDtypeStruct((rows, d), F32),
        compiler_params=_cparams(1),
        name="conv_mixer",
    )(x, x, x, mods, g.reshape(1, d), w_in, conv_w, w_out)


def kernel(x, c, ctx, c_ctx, mod_w, mod_b, norm_g, ffn_w_gate, ffn_w_up, ffn_w_down, attn_w_in, attn_w_out,
           na_rpb, diff_lambda, diff_subln_g, conv_w_in, conv_w_out, conv_w, final_g):
    batch, n_lat, d = x.shape
    n_ctx = ctx.shape[1]
    depth = mod_w.shape[0]
    assert n_ctx == ROW_TILE and n_lat % LAT_TILE == 0 and depth == 2 and batch + 1 <= 8
    tiles_per_batch = (n_lat + n_ctx) // ROW_TILE
    lat_tiles = n_lat // ROW_TILE
    bf = lambda a: a.astype(BF16)

    cvec = jnp.concatenate([c, c_ctx[None, :], jnp.zeros((8 - batch - 1, d), F32)], axis=0)
    mods_all = _mod_vectors(cvec, mod_w, mod_b).reshape(depth, 8, N_MOD, d)
    ffn_w = (bf(ffn_w_gate), bf(ffn_w_up), bf(ffn_w_down))

    def comb_group(i):
        return jnp.where(i % tiles_per_batch == lat_tiles, batch, i // tiles_per_batch)

    def comb_tile(i):
        return (i // lat_tiles) * tiles_per_batch + i % lat_tiles

    mods = mods_all[0]
    xc = _ffn(x.reshape(batch * n_lat, d), mods, norm_g[0, 0], ffn_w, 0, 0, tile=ROW_TILE,
              n_tiles=batch * tiles_per_batch, mod_base=0, group_of_tile=comb_group,
              x_tile_of=lambda i: (i // tiles_per_batch) * lat_tiles + jnp.minimum(i % tiles_per_batch, lat_tiles - 1),
              ctx=ctx.reshape(batch * n_ctx, d), ctx_period=tiles_per_batch)
    qa, ka, va, qb, kb, vbt = _attn_in(xc, mods, norm_g[0, 1], bf(attn_w_in[0]),
                                       _rope_tables(n_lat, n_ctx), batch=batch, tiles_per_batch=tiles_per_batch)
    lam_init = 0.8 - 0.6 * math.exp(-0.3 * 0)
    a_lat = _na_attention(qa, ka, va, _na_bias_tables(na_rpb[0], n_lat // GRID_W),
                          batch=batch, n_lat=n_lat, n_ctx=n_ctx)
    d_lat = _diff_attention(qb, kb, vbt, diff_lambda[0], diff_subln_g[0], batch=batch, n_q=n_lat, q_row0=0,
                            k_tile0=0, k_tiles=tiles_per_batch, tiles_per_batch=tiles_per_batch,
                            q_tile=DIFF_Q_TILE, lam_init=lam_init)
    w_out = bf(attn_w_out[0])
    x_lat = _ffn(xc, mods, norm_g[0, 2], ffn_w, 0, 1, tile=ROW_TILE, n_tiles=batch * lat_tiles, mod_base=6,
                 group_of_tile=lambda i: i // lat_tiles, x_tile_of=comb_tile,
                 attn=(a_lat, d_lat, w_out, lambda i: i))

    a_ctx = _dense_attention(qa, ka, va, batch=batch, q_row0=batch * n_lat, k_tile0=lat_tiles,
                             tiles_per_batch=tiles_per_batch)
    d_ctx = _diff_attention(qb, kb, vbt, diff_lambda[0], diff_subln_g[0], batch=batch, n_q=n_ctx,
                            q_row0=batch * n_lat, k_tile0=lat_tiles, k_tiles=1,
                            tiles_per_batch=tiles_per_batch, q_tile=ROW_TILE, lam_init=lam_init)
    x_ctx = _ffn(xc, mods, norm_g[0, 2], ffn_w, 0, 1, tile=ROW_TILE, n_tiles=batch, mod_base=6,
                 group_of_tile=lambda i: batch, x_tile_of=lambda i: i * tiles_per_batch + lat_tiles,
                 attn=(a_ctx, d_ctx, w_out, lambda i: i))
    del x_ctx

    mods = mods_all[1]
    n_lat_tiles = batch * n_lat // LAT_TILE
    lat_group = lambda i: i // (n_lat // LAT_TILE)
    x_lat = _ffn(x_lat, mods, norm_g[1, 0], ffn_w, 1, 0, tile=LAT_TILE, n_tiles=n_lat_tiles, mod_base=0,
                 group_of_tile=lat_group)
    x_lat = _conv_mixer(x_lat, mods, lat_group, norm_g[1, 1], bf(conv_w_in[0]), conv_w[0], bf(conv_w_out[0]),
                        tile=LAT_TILE, seq=n_lat)
    x_lat = _ffn(x_lat, mods, norm_g[1, 2], ffn_w, 1, 1, tile=LAT_TILE, n_tiles=n_lat_tiles, mod_base=6,
                 group_of_tile=lat_group, final_g=final_g)
    return x_lat.reshape(batch, n_lat, d)
```

```python
import functools
import math

import jax
import jax.numpy as jnp
import numpy as np
from jax import lax
from jax.experimental import pallas as pl
from jax.experimental.pallas import tpu as pltpu

F32 = jnp.float32
BF16 = jnp.bfloat16

D_MODEL = 1024
GRID_W = 64
HEAD_DIM = 64
NA_HEADS = 8
NA_WIN_ROWS = 8
NA_WIN_COLS = 16
DIFF_HEADS = 4
DIFF_V_DIM = 128
DIFF_V_ROWS = 144
LOG2_E = math.log2(math.e)
HALF_WIDTH = 512
ATTN_IN_WIDTH = 6 * HALF_WIDTH
CONV_WIDTH = 3
D_FF = 2816
ROPE_THETA = 10000.0
N_MOD = 9
EPS = 1e-6
NEG_INF = -1e30

LANES = 128
ROW_TILE = 256
LAT_TILE = 512
NA_Q_ROWS = 4
NA_BAND_ROWS = 12
DIFF_Q_TILE = 512
DIFF_K_TILE = 256
DIFF_CHUNK_TILES = 3
FF_CHUNKS = ((0, 1024), (1024, 1024), (2048, 768))
FFN_ROW_BLOCK = 256
VMEM_LIMIT = 56 * 1024 * 1024

NT_DIMS = (((1,), (1,)), ((), ()))


def _cparams(n_axes):
    return pltpu.CompilerParams(dimension_semantics=("arbitrary",) * n_axes,
                                vmem_limit_bytes=VMEM_LIMIT)


def _resident(shape, index_map):
    return pl.BlockSpec(shape, index_map, pipeline_mode=pl.Buffered(1))


def _modulate(x, g, shift, scale):
    ms = jnp.mean(x * x, axis=-1, keepdims=True)
    return x * lax.rsqrt(ms + EPS) * g * (1.0 + scale) + shift


def _silu(x):
    return x * (1.0 / (1.0 + jnp.exp(-x)))


def _mod_kernel(c_ref, w_ref, b_ref, o_ref):
    o_ref[0] = jnp.dot(_silu(c_ref[...]), w_ref[0], preferred_element_type=F32) + b_ref[0]


def _mod_vectors(cvec, mod_w, mod_b):
    depth, d, n = mod_w.shape
    bn = 1536
    return pl.pallas_call(
        _mod_kernel,
        grid=(depth, n // bn),
        in_specs=[pl.BlockSpec((8, d), lambda i, j: (0, 0)),
                  pl.BlockSpec((1, d, bn), lambda i, j: (i, 0, j)),
                  pl.BlockSpec((1, 1, bn), lambda i, j: (i, 0, j))],
        out_specs=pl.BlockSpec((1, 8, bn), lambda i, j: (i, 0, j)),
        out_shape=jax.ShapeDtypeStruct((depth, 8, n), F32),
        compiler_params=_cparams(2),
        name="adaln_vectors",
    )(cvec, mod_w, mod_b.reshape(depth, 1, n))


def _ffn_kernel(*refs, mod_base, ctx_period, attn, final):
    refs = list(refs)
    o_ref = refs.pop()
    x_ref = refs.pop(0)
    if ctx_period is not None:
        ctx_ref = refs.pop(0)
        is_ctx = pl.program_id(0) % ctx_period == ctx_period - 1
    if attn:
        a_ref, d_ref, wo_ref = refs.pop(0), refs.pop(0), refs.pop(0)
    mod_ref, g_ref, wg_ref, wu_ref, wd_ref = refs[:5]
    shift = mod_ref[0, mod_base:mod_base + 1, :]
    scale = mod_ref[0, mod_base + 1:mod_base + 2, :]
    gate = mod_ref[0, mod_base + 2:mod_base + 3, :]
    blocks = [slice(r, r + FFN_ROW_BLOCK) for r in range(0, x_ref.shape[0], FFN_ROW_BLOCK)]
    xs, xns = [], []
    for rs in blocks:
        x = x_ref[rs, :]
        if ctx_period is not None:
            x = jnp.where(is_ctx, ctx_ref[rs, :], x)
        if attn:
            y = (jnp.dot(a_ref[rs, :], wo_ref[0:HALF_WIDTH, :], preferred_element_type=F32)
                 + jnp.dot(d_ref[rs, :], wo_ref[HALF_WIDTH:2 * HALF_WIDTH, :], preferred_element_type=F32))
            x = x + mod_ref[0, 5:6, :] * y
        xs.append(x)
        xns.append(_modulate(x, g_ref[...], shift, scale).astype(BF16))
    for rs, x, xn in zip(blocks, xs, xns):
        y = None
        for lo, width in FF_CHUNKS:
            hg = jnp.dot(xn, wg_ref[:, lo:lo + width], preferred_element_type=F32)
            hu = jnp.dot(xn, wu_ref[:, lo:lo + width], preferred_element_type=F32)
            a = (_silu(hg) * hu).astype(BF16)
            part = jnp.dot(a, wd_ref[lo:lo + width, :], preferred_element_type=F32)
            y = part if y is None else y + part
        out = x + (0.5 * gate) * y
        if final:
            fg = refs[5][...]
            ms = jnp.mean(out * out, axis=-1, keepdims=True)
            out = out * lax.rsqrt(ms + EPS) * fg
        o_ref[rs, :] = out


def _ffn(x, mods, g, weights, layer, which, *, tile, n_tiles, mod_base, group_of_tile, x_tile_of=None,
         ctx=None, ctx_period=None, attn=None, final_g=None):
    d = x.shape[1]
    wg, wu, wd = weights
    f = wg.shape[-1]
    x_tile_of = x_tile_of or (lambda i: i)
    in_specs = [pl.BlockSpec((tile, d), lambda i: (x_tile_of(i), 0))]
    args = [x]
    if ctx is not None:
        in_specs.append(pl.BlockSpec((tile, d), lambda i: (i // ctx_period, 0)))
        args.append(ctx)
    if attn is not None:
        a, dd, w_out, ad_tile_of = attn
        in_specs += [pl.BlockSpec((tile, HALF_WIDTH), lambda i: (ad_tile_of(i), 0)),
                     pl.BlockSpec((tile, HALF_WIDTH), lambda i: (ad_tile_of(i), 0)),
                     _resident((2 * HALF_WIDTH, d), lambda i: (0, 0))]
        args += [a, dd, w_out]
    in_specs += [pl.BlockSpec((1, N_MOD, d), lambda i: (group_of_tile(i), 0, 0)),
                 _resident((1, d), lambda i: (0, 0)),
                 _resident((None, None, d, f), lambda i: (layer, which, 0, 0)),
                 _resident((None, None, d, f), lambda i: (layer, which, 0, 0)),
                 _resident((None, None, f, d), lambda i: (layer, which, 0, 0))]
    args += [mods, g.reshape(1, d), wg, wu, wd]
    if final_g is not None:
        in_specs.append(_resident((1, d), lambda i: (0, 0)))
        args.append(final_g.reshape(1, d))
    return pl.pallas_call(
        functools.partial(_ffn_kernel, mod_base=mod_base, ctx_period=ctx_period if ctx is not None else None,
                          attn=attn is not None, final=final_g is not None),
        grid=(n_tiles,),
        in_specs=in_specs,
        out_specs=pl.BlockSpec((tile, d), lambda i: (i, 0)),
        out_shape=jax.ShapeDtypeStruct((n_tiles * tile, d), F32),
        compiler_params=_cparams(1),
        name="macaron_ffn",
    )(*args)


def _rope(z, cos, sin_next, sin_prev):
    return z * cos + pltpu.roll(z, LANES - 1, 1) * sin_next + pltpu.roll(z, 1, 1) * sin_prev


def _attn_in_kernel(x_ref, mod_ref, g_ref, w_ref, cos_ref, sn_ref, sp_ref,
                    qa_ref, ka_ref, va_ref, qbt_ref, kb_ref, vbt_ref):
    x = x_ref[...]
    xn = _modulate(x, g_ref[...], mod_ref[0, 3:4, :], mod_ref[0, 4:5, :]).astype(BF16)
    w = HALF_WIDTH
    scale = HEAD_DIM ** -0.5 * LOG2_E
    qa_ref[...] = (jnp.dot(xn, w_ref[:, 0:w], preferred_element_type=F32) * scale).astype(BF16)
    ka_ref[...] = jnp.dot(xn, w_ref[:, w:2 * w], preferred_element_type=F32).astype(BF16)
    va_ref[...] = jnp.dot(xn, w_ref[:, 2 * w:3 * w], preferred_element_type=F32).astype(BF16)
    cos, sn, sp = cos_ref[...], sn_ref[...], sp_ref[...]
    qb = jnp.dot(xn, w_ref[:, 3 * w:4 * w], preferred_element_type=F32)
    kb = jnp.dot(xn, w_ref[:, 4 * w:5 * w], preferred_element_type=F32)
    for h in range(DIFF_HEADS):
        sl = slice(h * LANES, (h + 1) * LANES)
        qbt_ref[0, sl, :] = (_rope(qb[:, sl], cos, sn, sp) * scale).T.astype(BF16)
        kb_ref[:, sl] = _rope(kb[:, sl], cos, sn, sp).astype(BF16)
    vbt = jnp.dot(xn, w_ref[:, 5 * w:6 * w], preferred_element_type=F32).T.astype(BF16)
    pad_rows = DIFF_V_ROWS - DIFF_V_DIM
    ones_row = (lax.broadcasted_iota(jnp.int32, (pad_rows, vbt.shape[1]), 0) == 0).astype(BF16)
    for h in range(DIFF_HEADS):
        vbt_ref[0, h * DIFF_V_ROWS:h * DIFF_V_ROWS + DIFF_V_DIM, :] = vbt[h * DIFF_V_DIM:(h + 1) * DIFF_V_DIM]
        vbt_ref[0, h * DIFF_V_ROWS + DIFF_V_DIM:(h + 1) * DIFF_V_ROWS, :] = ones_row


def _rope_tables(n_lat, n_ctx):
    t = jnp.arange(n_lat)
    row = (t // GRID_W).astype(F32)
    col = (t % GRID_W).astype(F32)
    n_freq = HEAD_DIM // 4
    inv = ROPE_THETA ** (-jnp.arange(n_freq, dtype=F32) / n_freq)
    ang = jnp.concatenate([row[:, None] * inv, col[:, None] * inv], axis=-1)
    cos = jnp.repeat(jnp.cos(ang), 2, axis=-1)
    sin = jnp.repeat(jnp.sin(ang), 2, axis=-1)
    even = (jnp.arange(HEAD_DIM) % 2 == 0)[None, :]
    sin_next = jnp.where(even, -sin, 0.0)
    sin_prev = jnp.where(even, 0.0, sin)
    pad = lambda a, v: jnp.concatenate([a, jnp.full((n_ctx, HEAD_DIM), v, F32)], axis=0)
    two = lambda a: jnp.concatenate([a, a], axis=-1)
    return two(pad(cos, 1.0)), two(pad(sin_next, 0.0)), two(pad(sin_prev, 0.0))


def _attn_in(xc, mods, g, w_in, tables, *, batch, tiles_per_batch):
    rows, d = xc.shape
    lat_tiles = tiles_per_batch - 1
    n_tiles = rows // ROW_TILE

    def group(i):
        return jnp.where(i % tiles_per_batch == lat_tiles, batch, i // tiles_per_batch)

    def lat_first(i):
        b, j = i // tiles_per_batch, i % tiles_per_batch
        return jnp.where(j == lat_tiles, batch * lat_tiles + b, b * lat_tiles + j)

    tab_spec = pl.BlockSpec((ROW_TILE, LANES), lambda i: (i % tiles_per_batch, 0))
    half = jax.ShapeDtypeStruct((rows, HALF_WIDTH), BF16)
    q_spec = pl.BlockSpec((ROW_TILE, HALF_WIDTH), lambda i: (lat_first(i), 0))
    kv_spec = pl.BlockSpec((ROW_TILE, HALF_WIDTH), lambda i: (i, 0))
    return pl.pallas_call(
        _attn_in_kernel,
        grid=(n_tiles,),
        in_specs=[pl.BlockSpec((ROW_TILE, d), lambda i: (i, 0)),
                  pl.BlockSpec((1, N_MOD, d), lambda i: (group(i), 0, 0)),
                  _resident((1, d), lambda i: (0, 0)),
                  _resident((d, ATTN_IN_WIDTH), lambda i: (0, 0)),
                  tab_spec, tab_spec, tab_spec],
        out_specs=[q_spec, kv_spec, kv_spec,
                   pl.BlockSpec((1, HALF_WIDTH, ROW_TILE), lambda i: (lat_first(i), 0, 0)), kv_spec,
                   pl.BlockSpec((1, DIFF_HEADS * DIFF_V_ROWS, ROW_TILE), lambda i: (i, 0, 0))],
        out_shape=[half, half, half,
                   jax.ShapeDtypeStruct((n_tiles, HALF_WIDTH, ROW_TILE), BF16), half,
                   jax.ShapeDtypeStruct((n_tiles, DIFF_HEADS * DIFF_V_ROWS, ROW_TILE), BF16)],
        compiler_params=_cparams(1),
        name="attn_in_proj",
    )(xc, mods, g.reshape(1, d), w_in, *tables)


def _na_bias_tables(rpb, rows):
    heads = rpb.shape[0]
    kr_win = min(NA_WIN_ROWS, rows)
    qc = np.arange(GRID_W)
    col_start = np.clip(qc - NA_WIN_COLS // 2, 0, GRID_W - NA_WIN_COLS)
    col_ok = (qc[None, :] >= col_start[:, None]) & (qc[None, :] < col_start[:, None] + NA_WIN_COLS)
    dc = np.clip(qc[None, :] - qc[:, None] + NA_WIN_COLS - 1, 0, 2 * NA_WIN_COLS - 2)
    rpb = rpb.astype(F32) * LOG2_E
    t = jnp.full((heads, 2 * NA_WIN_ROWS - 1, GRID_W, GRID_W), NEG_INF, F32)
    for cidx in range(2 * NA_WIN_COLS - 1):
        t = jnp.where((col_ok & (dc == cidx))[None, None], rpb[:, :, cidx][:, :, None, None], t)
    neg = jnp.full((heads, GRID_W, GRID_W), NEG_INF, F32)
    out = []
    for r0, ws in ((0, 0), (NA_Q_ROWS, 0), (rows - NA_Q_ROWS, rows - NA_BAND_ROWS)):
        block_rows = []
        for i in range(NA_Q_ROWS):
            qr = r0 + i
            start = min(max(qr - kr_win // 2, 0), rows - kr_win)
            blocks = [t[:, kr - qr + NA_WIN_ROWS - 1] if start <= kr < start + kr_win else neg
                      for kr in range(ws, ws + NA_BAND_ROWS)]
            block_rows.append(jnp.concatenate(blocks, axis=-1))
        out.append(jnp.concatenate(block_rows, axis=1))
    return jnp.stack(out)


def _na_kernel(q_ref, k_ref, v_ref, bias_ref, o_ref, *, n_lat, n_ctx, rows):
    g = pl.program_id(1)
    ws = jnp.clip(NA_Q_ROWS * g - NA_WIN_ROWS // 2, 0, rows - NA_BAND_ROWS)
    start = pl.multiple_of(ws * GRID_W, GRID_W)
    band = NA_BAND_ROWS * GRID_W
    tq = NA_Q_ROWS * GRID_W
    low = lax.broadcasted_iota(jnp.int32, (tq, LANES), 1) < HEAD_DIM

    def scores(h):
        sl = slice(h // 2 * LANES, (h // 2 + 1) * LANES)
        q = q_ref[:, sl]
        qm = jnp.where(low if h % 2 == 0 else jnp.logical_not(low), q, jnp.zeros_like(q))
        sb = lax.dot_general(qm, k_ref[pl.ds(start, band), sl], NT_DIMS, preferred_element_type=F32)
        sc = lax.dot_general(qm, k_ref[n_lat:n_lat + n_ctx, sl], NT_DIMS, preferred_element_type=F32)
        return sb + bias_ref[0, h], sc

    def attend(h, sb, sc):
        sl = slice(h // 2 * LANES, (h // 2 + 1) * LANES)
        m = jnp.maximum(jnp.max(sb, axis=-1, keepdims=True), jnp.max(sc, axis=-1, keepdims=True))
        pb = jnp.exp2(sb - m)
        pc = jnp.exp2(sc - m)
        l = jnp.sum(pb, axis=-1, keepdims=True) + jnp.sum(pc, axis=-1, keepdims=True)
        o = (jnp.dot(pb.astype(BF16), v_ref[pl.ds(start, band), sl], preferred_element_type=F32)
             + jnp.dot(pc.astype(BF16), v_ref[n_lat:n_lat + n_ctx, sl], preferred_element_type=F32))
        return o * (1.0 / l)

    s_cur = scores(0)
    even_out = None
    for h in range(NA_HEADS):
        s_nxt = scores(h + 1) if h + 1 < NA_HEADS else None
        o = attend(h, *s_cur)
        if h % 2 == 0:
            even_out = o
        else:
            sl = slice(h // 2 * LANES, (h // 2 + 1) * LANES)
            o_ref[:, sl] = jnp.where(low, even_out, o).astype(BF16)
        s_cur = s_nxt


def _na_attention(qa, ka, va, bias, *, batch, n_lat, n_ctx):
    rows = n_lat // GRID_W
    groups = rows // NA_Q_ROWS
    tq = NA_Q_ROWS * GRID_W
    per_batch = n_lat + n_ctx

    def variant(g):
        return jnp.where(g == 0, 0, jnp.where(g == groups - 1, 2, 1))

    kv_spec = _resident((per_batch, HALF_WIDTH), lambda b, g: (b, 0))
    return pl.pallas_call(
        functools.partial(_na_kernel, n_lat=n_lat, n_ctx=n_ctx, rows=rows),
        grid=(batch, groups),
        in_specs=[pl.BlockSpec((tq, HALF_WIDTH), lambda b, g: (b * groups + g, 0)),
                  kv_spec, kv_spec,
                  pl.BlockSpec((1, NA_HEADS, tq, NA_BAND_ROWS * GRID_W), lambda b, g: (variant(g), 0, 0, 0))],
        out_specs=pl.BlockSpec((tq, HALF_WIDTH), lambda b, g: (b * groups + g, 0)),
        out_shape=jax.ShapeDtypeStruct((batch * n_lat, HALF_WIDTH), BF16),
        compiler_params=_cparams(2),
        name="neighbourhood_attention",
    )(qa, ka, va, bias)


def _diff_kernel(q_ref, k_ref, vt_ref, lam_ref, g_ref, o_ref, qc_ref, s_ref, m_ref, alpha_ref, acc_ref,
                 *, n_chunks, tiles_per_chunk, lam_init):
    tq = q_ref.shape[0] * q_ref.shape[2]
    tk = tiles_per_chunk * DIFF_K_TILE
    qt = jnp.concatenate([q_ref[j] for j in range(q_ref.shape[0])], axis=1)
    low = lax.broadcasted_iota(jnp.int32, qt.shape, 0) < HEAD_DIM
    zero = jnp.zeros_like(qt)
    qc_ref[:, 0:tq] = jnp.where(low, qt, zero)
    qc_ref[:, tq:2 * tq] = jnp.where(low, zero, qt)
    m_ref[...] = jnp.full(m_ref.shape, -jnp.inf, F32)
    acc_ref[...] = jnp.zeros(acc_ref.shape, F32)

    def tile_scores(c, t):
        row0 = pl.multiple_of(c * tk + t * DIFF_K_TILE, DIFF_K_TILE)
        s = jnp.dot(k_ref[pl.ds(row0, DIFF_K_TILE), :], qc_ref[...], preferred_element_type=F32)
        s_ref[t * DIFF_K_TILE:(t + 1) * DIFF_K_TILE, :] = s
        return jnp.max(s, axis=0, keepdims=True)

    def finish_max(cmax):
        m_old = m_ref[...]
        m_new = jnp.maximum(m_old, cmax)
        alpha_ref[...] = jnp.exp2(m_old - m_new)
        m_ref[...] = m_new

    def step(c, has_next):
        m = m_ref[...]
        alpha = alpha_ref[...]
        cmax = None
        for t in range(tiles_per_chunk):
            p = jnp.exp2(s_ref[t * DIFF_K_TILE:(t + 1) * DIFF_K_TILE, :] - m).astype(BF16)
            if has_next:
                tmax = tile_scores(c + 1, t)
                cmax = tmax if cmax is None else jnp.maximum(cmax, tmax)
            pv = jnp.dot(vt_ref[c * tiles_per_chunk + t], p, preferred_element_type=F32)
            acc_ref[...] = (alpha * acc_ref[...] if t == 0 else acc_ref[...]) + pv
        if has_next:
            finish_max(cmax)

    cmax = None
    for t in range(tiles_per_chunk):
        tmax = tile_scores(0, t)
        cmax = tmax if cmax is None else jnp.maximum(cmax, tmax)
    finish_max(cmax)

    def body(c, carry):
        step(c, True)
        return carry

    lax.fori_loop(0, n_chunks - 1, body, 0)
    step(n_chunks - 1, False)
    lv = lam_ref[...]
    lam = (jnp.exp(jnp.sum(lv[0:1] * lv[1:2], axis=-1, keepdims=True))
           - jnp.exp(jnp.sum(lv[2:3] * lv[3:4], axis=-1, keepdims=True)) + lam_init)
    o = acc_ref[0:DIFF_V_DIM, :] * (1.0 / acc_ref[DIFF_V_DIM:DIFF_V_DIM + 1, :])
    d = o[:, 0:tq] - lam * o[:, tq:2 * tq]
    ms = jnp.mean(d * d, axis=0, keepdims=True)
    d = d * lax.rsqrt(ms + EPS) * g_ref[...] * (1.0 - lam_init)
    o_ref[...] = d.T.astype(BF16)


def _diff_attention(qbt, kb, vbt, lam_vec, subln_g, *, batch, n_q, q_row0, k_tile0, k_tiles,
                    tiles_per_batch, q_tile, lam_init):
    q_tiles = n_q // q_tile
    q_blk0 = q_row0 // q_tile
    q_sub = q_tile // ROW_TILE
    tiles_per_chunk = DIFF_CHUNK_TILES if k_tiles % DIFF_CHUNK_TILES == 0 else 1
    n_chunks = k_tiles // tiles_per_chunk
    tk = tiles_per_chunk * DIFF_K_TILE
    if k_tile0 == 0:
        k_spec = _resident((k_tiles * DIFF_K_TILE, LANES), lambda b, h, i: (b, h))
    else:
        k_spec = _resident((k_tiles * DIFF_K_TILE, LANES), lambda b, h, i: (b * tiles_per_batch + k_tile0, h))
    return pl.pallas_call(
        functools.partial(_diff_kernel, n_chunks=n_chunks, tiles_per_chunk=tiles_per_chunk, lam_init=lam_init),
        grid=(batch, DIFF_HEADS, q_tiles),
        in_specs=[pl.BlockSpec((q_sub, LANES, ROW_TILE), lambda b, h, i: (q_blk0 + b * q_tiles + i, h, 0)),
                  k_spec,
                  _resident((k_tiles, DIFF_V_ROWS, DIFF_K_TILE),
                            lambda b, h, i: ((b * tiles_per_batch + k_tile0) // k_tiles, h, 0)),
                  _resident((4, HEAD_DIM), lambda b, h, i: (0, 0)),
                  _resident((DIFF_V_DIM, 1), lambda b, h, i: (0, 0))],
        out_specs=pl.BlockSpec((q_tile, LANES), lambda b, h, i: (b * q_tiles + i, h)),
        out_shape=jax.ShapeDtypeStruct((batch * n_q, HALF_WIDTH), BF16),
        scratch_shapes=[pltpu.VMEM((LANES, 2 * q_tile), BF16),
                        pltpu.VMEM((tk, 2 * q_tile), F32),
                        pltpu.VMEM((1, 2 * q_tile), F32),
                        pltpu.VMEM((1, 2 * q_tile), F32),
                        pltpu.VMEM((DIFF_V_ROWS, 2 * q_tile), F32)],
        compiler_params=_cparams(3),
        name="differential_attention",
    )(qbt, kb, vbt, lam_vec, subln_g.reshape(DIFF_V_DIM, 1))


def _dense_kernel(q_ref, k_ref, v_ref, o_ref):
    tq = q_ref.shape[0]
    low = lax.broadcasted_iota(jnp.int32, (tq, LANES), 1) < HEAD_DIM
    for p in range(NA_HEADS // 2):
        sl = slice(p * LANES, (p + 1) * LANES)
        q, k, v = q_ref[:, sl], k_ref[:, sl], v_ref[:, sl]
        outs = []
        for e in range(2):
            qm = jnp.where(low if e == 0 else jnp.logical_not(low), q, jnp.zeros_like(q))
            s = lax.dot_general(qm, k, NT_DIMS, preferred_element_type=F32)
            pr = jnp.exp2(s - jnp.max(s, axis=-1, keepdims=True))
            o = jnp.dot(pr.astype(BF16), v, preferred_element_type=F32)
            outs.append(o * (1.0 / jnp.sum(pr, axis=-1, keepdims=True)))
        o_ref[:, sl] = jnp.where(low, outs[0], outs[1]).astype(BF16)


def _dense_attention(qa, ka, va, *, batch, q_row0, k_tile0, tiles_per_batch):
    q_blk0 = q_row0 // ROW_TILE
    kv_spec = pl.BlockSpec((ROW_TILE, HALF_WIDTH), lambda b: (b * tiles_per_batch + k_tile0, 0))
    return pl.pallas_call(
        _dense_kernel,
        grid=(batch,),
        in_specs=[pl.BlockSpec((ROW_TILE, HALF_WIDTH), lambda b: (q_blk0 + b, 0)), kv_spec, kv_spec],
        out_specs=pl.BlockSpec((ROW_TILE, HALF_WIDTH), lambda b: (b, 0)),
        out_shape=jax.ShapeDtypeStruct((batch * ROW_TILE, HALF_WIDTH), BF16),
        compiler_params=_cparams(1),
        name="context_attention",
    )(qa, ka, va)


def _conv_kernel(x_ref, xp_ref, xs_ref, mod_ref, g_ref, w_in_ref, cw_ref, w_out_ref, o_ref, *, tiles_per_seq):
    i = pl.program_id(0)
    tile, d = x_ref.shape
    halo = xp_ref.shape[0]
    x = x_ref[...]
    x_ext = jnp.concatenate([xp_ref[...], x, xs_ref[...]], axis=0)
    xn = _modulate(x_ext, g_ref[...], mod_ref[0, 3:4, :], mod_ref[0, 4:5, :]).astype(BF16)
    cg = jnp.dot(xn, w_in_ref[:, d:2 * d], preferred_element_type=F32)
    hh = jnp.dot(xn, w_in_ref[:, 2 * d:3 * d], preferred_element_type=F32)
    u = cg * hh
    n_ext = tile + 2 * halo
    ridx = lax.broadcasted_iota(jnp.int32, (n_ext, 1), 0)
    first = (i % tiles_per_seq) == 0
    last = (i % tiles_per_seq) == tiles_per_seq - 1
    u = jnp.where(jnp.logical_or(jnp.logical_and(first, ridx < halo),
                                 jnp.logical_and(last, ridx >= halo + tile)), 0.0, u)
    cw = cw_ref[...]
    u_prev = pltpu.roll(u, 1, 0)[halo:halo + tile]
    u_next = pltpu.roll(u, n_ext - 1, 0)[halo:halo + tile]
    v = u_prev * cw[0:1] + u[halo:halo + tile] * cw[1:2] + u_next * cw[2:3]
    bg = jnp.dot(xn[halo:halo + tile], w_in_ref[:, 0:d], preferred_element_type=F32)
    y = jnp.dot((bg * v).astype(BF16), w_out_ref[...], preferred_element_type=F32)
    o_ref[...] = x + mod_ref[0, 5:6, :] * y


def _conv_mixer(x, mods, group_of_tile, g, w_in, conv_w, w_out, *, tile, seq):
    rows, d = x.shape
    halo = 8
    tiles_per_seq = seq // tile
    hb = tile // halo
    n_hblocks = rows // halo
    return pl.pallas_call(
        functools.partial(_conv_kernel, tiles_per_seq=tiles_per_seq),
        grid=(rows // tile,),
        in_specs=[pl.BlockSpec((tile, d), lambda i: (i, 0)),
                  pl.BlockSpec((halo, d), lambda i: (jnp.maximum(i * hb - 1, 0), 0)),
                  pl.BlockSpec((halo, d), lambda i: (jnp.minimum((i + 1) * hb, n_hblocks - 1), 0)),
                  pl.BlockSpec((1, N_MOD, d), lambda i: (group_of_tile(i), 0, 0)),
                  _resident((1, d), lambda i: (0, 0)),
                  _resident((d, 3 * d), lambda i: (0, 0)),
                  _resident((CONV_WIDTH, d), lambda i: (0, 0)),
                  _resident((d, d), lambda i: (0, 0))],
        out_specs=pl.BlockSpec((tile, d), lambda i: (i, 0)),
        out_shape=jax.ShapeDtypeStruct((rows, d), F32),
        compiler_params=_cparams(1),
        name="conv_mixer",
    )(x, x, x, mods, g.reshape(1, d), w_in, conv_w, w_out)


def kernel(x, c, ctx, c_ctx, mod_w, mod_b, norm_g, ffn_w_gate, ffn_w_up, ffn_w_down, attn_w_in, attn_w_out,
           na_rpb, diff_lambda, diff_subln_g, conv_w_in, conv_w_out, conv_w, final_g):
    batch, n_lat, d = x.shape
    n_ctx = ctx.shape[1]
    depth = mod_w.shape[0]
    assert n_ctx == ROW_TILE and n_lat % LAT_TILE == 0 and depth == 2 and batch + 1 <= 8
    tiles_per_batch = (n_lat + n_ctx) // ROW_TILE
    lat_tiles = n_lat // ROW_TILE
    bf = lambda a: a.astype(BF16)

    cvec = jnp.concatenate([c, c_ctx[None, :], jnp.zeros((8 - batch - 1, d), F32)], axis=0)
    mods_all = _mod_vectors(cvec, mod_w, mod_b).reshape(depth, 8, N_MOD, d)
    ffn_w = (bf(ffn_w_gate), bf(ffn_w_up), bf(ffn_w_down))

    def comb_group(i):
        return jnp.where(i % tiles_per_batch == lat_tiles, batch, i // tiles_per_batch)

    def comb_tile(i):
        return (i // lat_tiles) * tiles_per_batch + i % lat_tiles

    mods = mods_all[0]
    xc = _ffn(x.reshape(batch * n_lat, d), mods, norm_g[0, 0], ffn_w, 0, 0, tile=ROW_TILE,
              n_tiles=batch * tiles_per_batch, mod_base=0, group_of_tile=comb_group,
              x_tile_of=lambda i: (i // tiles_per_batch) * lat_tiles + jnp.minimum(i % tiles_per_batch, lat_tiles - 1),
              ctx=ctx.reshape(batch * n_ctx, d), ctx_period=tiles_per_batch)
    qa, ka, va, qbt, kb, vbt = _attn_in(xc, mods, norm_g[0, 1], bf(attn_w_in[0]),
                                       _rope_tables(n_lat, n_ctx), batch=batch, tiles_per_batch=tiles_per_batch)
    lam_init = 0.8 - 0.6 * math.exp(-0.3 * 0)
    a_lat = _na_attention(qa, ka, va, _na_bias_tables(na_rpb[0], n_lat // GRID_W),
                          batch=batch, n_lat=n_lat, n_ctx=n_ctx)
    d_lat = _diff_attention(qbt, kb, vbt, diff_lambda[0], diff_subln_g[0], batch=batch, n_q=n_lat, q_row0=0,
                            k_tile0=0, k_tiles=tiles_per_batch, tiles_per_batch=tiles_per_batch,
                            q_tile=DIFF_Q_TILE, lam_init=lam_init)
    w_out = bf(attn_w_out[0])
    x_lat = _ffn(xc, mods, norm_g[0, 2], ffn_w, 0, 1, tile=ROW_TILE, n_tiles=batch * lat_tiles, mod_base=6,
                 group_of_tile=lambda i: i // lat_tiles, x_tile_of=comb_tile,
                 attn=(a_lat, d_lat, w_out, lambda i: i))

    a_ctx = _dense_attention(qa, ka, va, batch=batch, q_row0=batch * n_lat, k_tile0=lat_tiles,
                             tiles_per_batch=tiles_per_batch)
    d_ctx = _diff_attention(qbt, kb, vbt, diff_lambda[0], diff_subln_g[0], batch=batch, n_q=n_ctx,
                            q_row0=batch * n_lat, k_tile0=lat_tiles, k_tiles=1,
                            tiles_per_batch=tiles_per_batch, q_tile=ROW_TILE, lam_init=lam_init)
    x_ctx = _ffn(xc, mods, norm_g[0, 2], ffn_w, 0, 1, tile=ROW_TILE, n_tiles=batch, mod_base=6,
                 group_of_tile=lambda i: batch, x_tile_of=lambda i: i * tiles_per_batch + lat_tiles,
                 attn=(a_ctx, d_ctx, w_out, lambda i: i))
    del x_ctx

    mods = mods_all[1]
    n_lat_tiles = batch * n_lat // LAT_TILE
    lat_group = lambda i: i // (n_lat // LAT_TILE)
    x_lat = _ffn(x_lat, mods, norm_g[1, 0], ffn_w, 1, 0, tile=LAT_TILE, n_tiles=n_lat_tiles, mod_base=0,
                 group_of_tile=lat_group)
    x_lat = _conv_mixer(x_lat, mods, lat_group, norm_g[1, 1], bf(conv_w_in[0]), conv_w[0], bf(conv_w_out[0]),
                        tile=LAT_TILE, seq=n_lat)
    x_lat = _ffn(x_lat, mods, norm_g[1, 2], ffn_w, 1, 1, tile=LAT_TILE, n_tiles=n_lat_tiles, mod_base=6,
                 group_of_tile=lat_group, final_g=final_g)
    return x_lat.reshape(batch, n_lat, d)
```

```python
import functools
import math

import jax
import jax.numpy as jnp
import numpy as np
from jax import lax
from jax.experimental import pallas as pl
from jax.experimental.pallas import tpu as pltpu

F32 = jnp.float32
BF16 = jnp.bfloat16

D_MODEL = 1024
GRID_W = 64
HEAD_DIM = 64
NA_HEADS = 8
NA_WIN_ROWS = 8
NA_WIN_COLS = 16
DIFF_HEADS = 4
DIFF_V_DIM = 128
DIFF_V_ROWS = 144
LOG2_E = math.log2(math.e)
HALF_WIDTH = 512
ATTN_IN_WIDTH = 6 * HALF_WIDTH
CONV_WIDTH = 3
D_FF = 2816
ROPE_THETA = 10000.0
N_MOD = 9
EPS = 1e-6
NEG_INF = -1e30

LANES = 128
ROW_TILE = 256
LAT_TILE = 512
NA_Q_ROWS = 4
NA_BAND_ROWS = 12
DIFF_Q_TILE = 512
DIFF_K_TILE = 256
DIFF_CHUNK_TILES = 3
FF_CHUNKS = ((0, 1024), (1024, 1024), (2048, 768))
FFN_ROW_BLOCK = 256
FF_CAST_CHUNK = 256
VMEM_LIMIT = 56 * 1024 * 1024

NT_DIMS = (((1,), (1,)), ((), ()))


def _cparams(n_axes):
    return pltpu.CompilerParams(dimension_semantics=("arbitrary",) * n_axes,
                                vmem_limit_bytes=VMEM_LIMIT)


def _resident(shape, index_map):
    return pl.BlockSpec(shape, index_map, pipeline_mode=pl.Buffered(1))


def _modulate(x, g, shift, scale):
    ms = jnp.mean(x * x, axis=-1, keepdims=True)
    return x * lax.rsqrt(ms + EPS) * g * (1.0 + scale) + shift


def _silu(x):
    return x * (1.0 / (1.0 + jnp.exp(-x)))


def _mod_kernel(c_ref, w_ref, b_ref, o_ref):
    o_ref[0] = jnp.dot(_silu(c_ref[...]), w_ref[0], preferred_element_type=F32) + b_ref[0]


def _mod_vectors(cvec, mod_w, mod_b):
    depth, d, n = mod_w.shape
    bn = 1536
    return pl.pallas_call(
        _mod_kernel,
        grid=(depth, n // bn),
        in_specs=[pl.BlockSpec((8, d), lambda i, j: (0, 0)),
                  pl.BlockSpec((1, d, bn), lambda i, j: (i, 0, j)),
                  pl.BlockSpec((1, 1, bn), lambda i, j: (i, 0, j))],
        out_specs=pl.BlockSpec((1, 8, bn), lambda i, j: (i, 0, j)),
        out_shape=jax.ShapeDtypeStruct((depth, 8, n), F32),
        compiler_params=_cparams(2),
        name="adaln_vectors",
    )(cvec, mod_w, mod_b.reshape(depth, 1, n))


def _ffn_kernel(*refs, mod_base, attn, final, n_cast):
    refs = list(refs)
    wg_ref, wu_ref, wd_ref = refs[-3:]
    o_ref = refs[-4]
    x_ref = refs.pop(0)
    if attn:
        a_ref, d_ref, wo_ref = refs.pop(0), refs.pop(0), refs.pop(0)
    mod_ref, g_ref, wg32_ref, wu32_ref, wd32_ref = refs[:5]
    step = pl.program_id(0)

    for c in range(n_cast):
        @pl.when(step == c)
        def _(c=c):
            sl = slice(c * FF_CAST_CHUNK, (c + 1) * FF_CAST_CHUNK)
            wg_ref[:, sl] = wg32_ref[...].astype(BF16)
            wu_ref[:, sl] = wu32_ref[...].astype(BF16)
            wd_ref[sl, :] = wd32_ref[...].astype(BF16)

    @pl.when(step >= n_cast)
    def _():
        shift = mod_ref[0, mod_base:mod_base + 1, :]
        scale = mod_ref[0, mod_base + 1:mod_base + 2, :]
        gate = mod_ref[0, mod_base + 2:mod_base + 3, :]
        blocks = [slice(r, r + FFN_ROW_BLOCK) for r in range(0, x_ref.shape[0], FFN_ROW_BLOCK)]
        xs, xns = [], []
        for rs in blocks:
            x = x_ref[rs, :]
            if attn:
                y = (jnp.dot(a_ref[rs, :], wo_ref[0:HALF_WIDTH, :], preferred_element_type=F32)
                     + jnp.dot(d_ref[rs, :], wo_ref[HALF_WIDTH:2 * HALF_WIDTH, :], preferred_element_type=F32))
                x = x + mod_ref[0, 5:6, :] * y
            xs.append(x)
            xns.append(_modulate(x, g_ref[...], shift, scale).astype(BF16))
        for rs, x, xn in zip(blocks, xs, xns):
            y = None
            for lo, width in FF_CHUNKS:
                hg = jnp.dot(xn, wg_ref[:, lo:lo + width], preferred_element_type=F32)
                hu = jnp.dot(xn, wu_ref[:, lo:lo + width], preferred_element_type=F32)
                a = (_silu(hg) * hu).astype(BF16)
                part = jnp.dot(a, wd_ref[lo:lo + width, :], preferred_element_type=F32)
                y = part if y is None else y + part
            out = x + (0.5 * gate) * y
            if final:
                fg = refs[5][...]
                ms = jnp.mean(out * out, axis=-1, keepdims=True)
                out = out * lax.rsqrt(ms + EPS) * fg
            o_ref[rs, :] = out


def _ffn(x, mods, g, weights, layer, which, *, tile, mod_base, group_of_tile, attn=None, final_g=None):
    rows, d = x.shape
    n_tiles = rows // tile
    wg, wu, wd = weights
    f = wg.shape[-1]
    n_cast = f // FF_CAST_CHUNK
    row_tile = lambda s: jnp.maximum(s - n_cast, 0)
    slab = lambda s: jnp.minimum(s, n_cast - 1)
    in_specs = [pl.BlockSpec((tile, d), lambda s: (row_tile(s), 0))]
    args = [x]
    if attn is not None:
        a, dd, w_out = attn
        in_specs += [pl.BlockSpec((tile, HALF_WIDTH), lambda s: (row_tile(s), 0)),
                     pl.BlockSpec((tile, HALF_WIDTH), lambda s: (row_tile(s), 0)),
                     _resident((2 * HALF_WIDTH, d), lambda s: (0, 0))]
        args += [a, dd, w_out]
    in_specs += [pl.BlockSpec((1, N_MOD, d), lambda s: (group_of_tile(row_tile(s)), 0, 0)),
                 _resident((1, d), lambda s: (0, 0)),
                 pl.BlockSpec((None, None, d, FF_CAST_CHUNK), lambda s: (layer, which, 0, slab(s))),
                 pl.BlockSpec((None, None, d, FF_CAST_CHUNK), lambda s: (layer, which, 0, slab(s))),
                 pl.BlockSpec((None, None, FF_CAST_CHUNK, d), lambda s: (layer, which, slab(s), 0))]
    args += [mods, g.reshape(1, d), wg, wu, wd]
    if final_g is not None:
        in_specs.append(_resident((1, d), lambda s: (0, 0)))
        args.append(final_g.reshape(1, d))
    return pl.pallas_call(
        functools.partial(_ffn_kernel, mod_base=mod_base, attn=attn is not None, final=final_g is not None,
                          n_cast=n_cast),
        grid=(n_cast + n_tiles,),
        in_specs=in_specs,
        out_specs=pl.BlockSpec((tile, d), lambda s: (row_tile(s), 0)),
        out_shape=jax.ShapeDtypeStruct((rows, d), F32),
        scratch_shapes=[pltpu.VMEM((d, f), BF16), pltpu.VMEM((d, f), BF16), pltpu.VMEM((f, d), BF16)],
        compiler_params=_cparams(1),
        name="macaron_ffn",
    )(*args)


def _rope(z, cos, sin_next, sin_prev):
    return z * cos + pltpu.roll(z, LANES - 1, 1) * sin_next + pltpu.roll(z, 1, 1) * sin_prev


def _rope_tile_tables(row_ref, col_ref, is_ctx):
    grid_rows = ROW_TILE // GRID_W
    lane = lax.broadcasted_iota(jnp.int32, (ROW_TILE, LANES), 1)
    by_row = (lane % HEAD_DIM) // 2 < HEAD_DIM // 4
    tabs = []
    for k, identity in enumerate((1.0, 0.0, 0.0)):
        rows = row_ref[k, 0]
        row_part = jnp.concatenate([jnp.broadcast_to(rows[r:r + 1], (GRID_W, LANES)) for r in range(grid_rows)],
                                   axis=0)
        col_part = jnp.concatenate([col_ref[k]] * grid_rows, axis=0)
        tabs.append(jnp.where(is_ctx, identity, jnp.where(by_row, row_part, col_part)))
    return tabs


def _attn_in_kernel(x_ref, xc_ref, mod_ref, g_ref, w_ref, row_ref, col_ref,
                    qa_ref, ka_ref, va_ref, qbt_ref, kb_ref, vbt_ref, *, tiles_per_batch):
    is_ctx = pl.program_id(0) % tiles_per_batch == tiles_per_batch - 1
    x = jnp.where(is_ctx, xc_ref[...], x_ref[...])
    xn = _modulate(x, g_ref[...], mod_ref[0, 3:4, :], mod_ref[0, 4:5, :]).astype(BF16)
    w = HALF_WIDTH
    scale = HEAD_DIM ** -0.5 * LOG2_E
    qa_ref[...] = (jnp.dot(xn, w_ref[:, 0:w], preferred_element_type=F32) * scale).astype(BF16)
    ka_ref[...] = jnp.dot(xn, w_ref[:, w:2 * w], preferred_element_type=F32).astype(BF16)
    va_ref[...] = jnp.dot(xn, w_ref[:, 2 * w:3 * w], preferred_element_type=F32).astype(BF16)
    cos, sn, sp = _rope_tile_tables(row_ref, col_ref, is_ctx)
    qb = jnp.dot(xn, w_ref[:, 3 * w:4 * w], preferred_element_type=F32)
    kb = jnp.dot(xn, w_ref[:, 4 * w:5 * w], preferred_element_type=F32)
    for h in range(DIFF_HEADS):
        sl = slice(h * LANES, (h + 1) * LANES)
        qbt_ref[0, sl, :] = (_rope(qb[:, sl], cos, sn, sp) * scale).T.astype(BF16)
        kb_ref[:, sl] = _rope(kb[:, sl], cos, sn, sp).astype(BF16)
    vbt = jnp.dot(xn, w_ref[:, 5 * w:6 * w], preferred_element_type=F32).T.astype(BF16)
    pad_rows = DIFF_V_ROWS - DIFF_V_DIM
    ones_row = (lax.broadcasted_iota(jnp.int32, (pad_rows, vbt.shape[1]), 0) == 0).astype(BF16)
    for h in range(DIFF_HEADS):
        vbt_ref[0, h * DIFF_V_ROWS:h * DIFF_V_ROWS + DIFF_V_DIM, :] = vbt[h * DIFF_V_DIM:(h + 1) * DIFF_V_DIM]
        vbt_ref[0, h * DIFF_V_ROWS + DIFF_V_DIM:(h + 1) * DIFF_V_ROWS, :] = ones_row


def _rope_tables(n_lat):
    n_freq = HEAD_DIM // 4
    grid_rows = ROW_TILE // GRID_W
    lane = jnp.arange(LANES)
    pair = (lane % HEAD_DIM) // 2
    inv = ROPE_THETA ** (-(pair % n_freq).astype(F32) / n_freq)
    even = lane % 2 == 0

    def tables(pos):
        ang = pos.astype(F32)[:, None] * inv[None, :]
        sin = jnp.sin(ang)
        return jnp.stack([jnp.cos(ang), jnp.where(even, -sin, 0.0), jnp.where(even, 0.0, sin)])

    n_rows = n_lat // GRID_W
    row_tabs = tables(jnp.arange(n_rows)).reshape(3, n_rows // grid_rows, grid_rows, LANES)
    row_tabs = jnp.pad(row_tabs, ((0, 0), (0, 0), (0, 8 - grid_rows), (0, 0)))
    return row_tabs, tables(jnp.arange(GRID_W))


def _attn_in(x_lat, x_ctx, mods, g, w_in, tables, *, batch, tiles_per_batch):
    d = x_lat.shape[1]
    lat_tiles = tiles_per_batch - 1
    n_tiles = batch * tiles_per_batch
    rows = n_tiles * ROW_TILE

    def group(i):
        return jnp.where(i % tiles_per_batch == lat_tiles, batch, i // tiles_per_batch)

    def lat_first(i):
        b, j = i // tiles_per_batch, i % tiles_per_batch
        return jnp.where(j == lat_tiles, batch * lat_tiles + b, b * lat_tiles + j)

    half = jax.ShapeDtypeStruct((rows, HALF_WIDTH), BF16)
    q_spec = pl.BlockSpec((ROW_TILE, HALF_WIDTH), lambda i: (lat_first(i), 0))
    kv_spec = pl.BlockSpec((ROW_TILE, HALF_WIDTH), lambda i: (i, 0))
    def lat_tile(i):
        return (i // tiles_per_batch) * lat_tiles + jnp.minimum(i % tiles_per_batch, lat_tiles - 1)

    return pl.pallas_call(
        functools.partial(_attn_in_kernel, tiles_per_batch=tiles_per_batch),
        grid=(n_tiles,),
        in_specs=[pl.BlockSpec((ROW_TILE, d), lambda i: (lat_tile(i), 0)),
                  pl.BlockSpec((ROW_TILE, d), lambda i: (i // tiles_per_batch, 0)),
                  pl.BlockSpec((1, N_MOD, d), lambda i: (group(i), 0, 0)),
                  _resident((1, d), lambda i: (0, 0)),
                  _resident((d, ATTN_IN_WIDTH), lambda i: (0, 0)),
                  pl.BlockSpec((3, 1, 8, LANES), lambda i: (0, jnp.minimum(i % tiles_per_batch, lat_tiles - 1), 0, 0)),
                  _resident((3, GRID_W, LANES), lambda i: (0, 0, 0))],
        out_specs=[q_spec, kv_spec, kv_spec,
                   pl.BlockSpec((1, HALF_WIDTH, ROW_TILE), lambda i: (lat_first(i), 0, 0)), kv_spec,
                   pl.BlockSpec((1, DIFF_HEADS * DIFF_V_ROWS, ROW_TILE), lambda i: (i, 0, 0))],
        out_shape=[half, half, half,
                   jax.ShapeDtypeStruct((n_tiles, HALF_WIDTH, ROW_TILE), BF16), half,
                   jax.ShapeDtypeStruct((n_tiles, DIFF_HEADS * DIFF_V_ROWS, ROW_TILE), BF16)],
        compiler_params=_cparams(1),
        name="attn_in_proj",
    )(x_lat, x_ctx, mods, g.reshape(1, d), w_in, *tables)


def _na_bias_tables(rpb, rows):
    heads = rpb.shape[0]
    kr_win = min(NA_WIN_ROWS, rows)
    qc = np.arange(GRID_W)
    col_start = np.clip(qc - NA_WIN_COLS // 2, 0, GRID_W - NA_WIN_COLS)
    col_ok = (qc[None, :] >= col_start[:, None]) & (qc[None, :] < col_start[:, None] + NA_WIN_COLS)
    rpb = rpb.astype(F32) * LOG2_E
    n_dr, n_dc = 2 * NA_WIN_ROWS - 1, 2 * NA_WIN_COLS - 1
    left = GRID_W - NA_WIN_COLS
    u = jnp.pad(rpb, ((0, 0), (0, 0), (left, 2 * GRID_W - n_dc - left)), constant_values=NEG_INF)
    u = jnp.broadcast_to(u[:, :, None, :], (heads, n_dr, GRID_W, 2 * GRID_W)).reshape(heads, n_dr, -1)
    t = u[:, :, GRID_W - 1:GRID_W - 1 + GRID_W * (2 * GRID_W - 1)].reshape(heads, n_dr, GRID_W, 2 * GRID_W - 1)
    t = jnp.where(col_ok[None, None], t[..., :GRID_W], NEG_INF)
    neg = jnp.full((heads, GRID_W, GRID_W), NEG_INF, F32)
    out = []
    for r0, ws in ((0, 0), (NA_Q_ROWS, 0), (rows - NA_Q_ROWS, rows - NA_BAND_ROWS)):
        block_rows = []
        for i in range(NA_Q_ROWS):
            qr = r0 + i
            start = min(max(qr - kr_win // 2, 0), rows - kr_win)
            blocks = [t[:, kr - qr + NA_WIN_ROWS - 1] if start <= kr < start + kr_win else neg
                      for kr in range(ws, ws + NA_BAND_ROWS)]
            block_rows.append(jnp.concatenate(blocks, axis=-1))
        out.append(jnp.concatenate(block_rows, axis=1))
    return jnp.stack(out)


def _na_kernel(q_ref, k_ref, v_ref, bias_ref, o_ref, *, n_lat, n_ctx, rows):
    g = pl.program_id(1)
    ws = jnp.clip(NA_Q_ROWS * g - NA_WIN_ROWS // 2, 0, rows - NA_BAND_ROWS)
    start = pl.multiple_of(ws * GRID_W, GRID_W)
    band = NA_BAND_ROWS * GRID_W
    tq = NA_Q_ROWS * GRID_W
    low = lax.broadcasted_iota(jnp.int32, (tq, LANES), 1) < HEAD_DIM

    def scores(h):
        sl = slice(h // 2 * LANES, (h // 2 + 1) * LANES)
        q = q_ref[:, sl]
        qm = jnp.where(low if h % 2 == 0 else jnp.logical_not(low), q, jnp.zeros_like(q))
        sb = lax.dot_general(qm, k_ref[pl.ds(start, band), sl], NT_DIMS, preferred_element_type=F32)
        sc = lax.dot_general(qm, k_ref[n_lat:n_lat + n_ctx, sl], NT_DIMS, preferred_element_type=F32)
        return sb + bias_ref[0, h], sc

    def attend(h, sb, sc):
        sl = slice(h // 2 * LANES, (h // 2 + 1) * LANES)
        m = jnp.maximum(jnp.max(sb, axis=-1, keepdims=True), jnp.max(sc, axis=-1, keepdims=True))
        pb = jnp.exp2(sb - m)
        pc = jnp.exp2(sc - m)
        l = jnp.sum(pb, axis=-1, keepdims=True) + jnp.sum(pc, axis=-1, keepdims=True)
        o = (jnp.dot(pb.astype(BF16), v_ref[pl.ds(start, band), sl], preferred_element_type=F32)
             + jnp.dot(pc.astype(BF16), v_ref[n_lat:n_lat + n_ctx, sl], preferred_element_type=F32))
        return o * (1.0 / l)

    s_cur = scores(0)
    even_out = None
    for h in range(NA_HEADS):
        s_nxt = scores(h + 1) if h + 1 < NA_HEADS else None
        o = attend(h, *s_cur)
        if h % 2 == 0:
            even_out = o
        else:
            sl = slice(h // 2 * LANES, (h // 2 + 1) * LANES)
            o_ref[:, sl] = jnp.where(low, even_out, o).astype(BF16)
        s_cur = s_nxt


def _na_attention(qa, ka, va, bias, *, batch, n_lat, n_ctx):
    rows = n_lat // GRID_W
    groups = rows // NA_Q_ROWS
    tq = NA_Q_ROWS * GRID_W
    per_batch = n_lat + n_ctx

    def variant(g):
        return jnp.where(g == 0, 0, jnp.where(g == groups - 1, 2, 1))

    kv_spec = _resident((per_batch, HALF_WIDTH), lambda b, g: (b, 0))
    return pl.pallas_call(
        functools.partial(_na_kernel, n_lat=n_lat, n_ctx=n_ctx, rows=rows),
        grid=(batch, groups),
        in_specs=[pl.BlockSpec((tq, HALF_WIDTH), lambda b, g: (b * groups + g, 0)),
                  kv_spec, kv_spec,
                  pl.BlockSpec((1, NA_HEADS, tq, NA_BAND_ROWS * GRID_W), lambda b, g: (variant(g), 0, 0, 0))],
        out_specs=pl.BlockSpec((tq, HALF_WIDTH), lambda b, g: (b * groups + g, 0)),
        out_shape=jax.ShapeDtypeStruct((batch * n_lat, HALF_WIDTH), BF16),
        compiler_params=_cparams(2),
        name="neighbourhood_attention",
    )(qa, ka, va, bias)


def _diff_kernel(q_ref, k_ref, vt_ref, lam_ref, g_ref, o_ref, qc_ref, s_ref, m_ref, alpha_ref, acc_ref,
                 *, n_chunks, tiles_per_chunk, lam_init):
    tq = q_ref.shape[0] * q_ref.shape[2]
    tk = tiles_per_chunk * DIFF_K_TILE
    qt = jnp.concatenate([q_ref[j] for j in range(q_ref.shape[0])], axis=1)
    low = lax.broadcasted_iota(jnp.int32, qt.shape, 0) < HEAD_DIM
    zero = jnp.zeros_like(qt)
    qc_ref[:, 0:tq] = jnp.where(low, qt, zero)
    qc_ref[:, tq:2 * tq] = jnp.where(low, zero, qt)
    m_ref[...] = jnp.full(m_ref.shape, -jnp.inf, F32)
    acc_ref[...] = jnp.zeros(acc_ref.shape, F32)

    def tile_scores(c, t):
        row0 = pl.multiple_of(c * tk + t * DIFF_K_TILE, DIFF_K_TILE)
        s = jnp.dot(k_ref[pl.ds(row0, DIFF_K_TILE), :], qc_ref[...], preferred_element_type=F32)
        s_ref[t * DIFF_K_TILE:(t + 1) * DIFF_K_TILE, :] = s
        return jnp.max(s, axis=0, keepdims=True)

    def finish_max(cmax):
        m_old = m_ref[...]
        m_new = jnp.maximum(m_old, cmax)
        alpha_ref[...] = jnp.exp2(m_old - m_new)
        m_ref[...] = m_new

    def step(c, has_next):
        m = m_ref[...]
        alpha = alpha_ref[...]
        cmax = None
        for t in range(tiles_per_chunk):
            p = jnp.exp2(s_ref[t * DIFF_K_TILE:(t + 1) * DIFF_K_TILE, :] - m).astype(BF16)
            if has_next:
                tmax = tile_scores(c + 1, t)
                cmax = tmax if cmax is None else jnp.maximum(cmax, tmax)
            pv = jnp.dot(vt_ref[c * tiles_per_chunk + t], p, preferred_element_type=F32)
            acc_ref[...] = (alpha * acc_ref[...] if t == 0 else acc_ref[...]) + pv
        if has_next:
            finish_max(cmax)

    cmax = None
    for t in range(tiles_per_chunk):
        tmax = tile_scores(0, t)
        cmax = tmax if cmax is None else jnp.maximum(cmax, tmax)
    finish_max(cmax)

    def body(c, carry):
        step(c, True)
        return carry

    lax.fori_loop(0, n_chunks - 1, body, 0)
    step(n_chunks - 1, False)
    lv = lam_ref[...]
    lam = (jnp.exp(jnp.sum(lv[0:1] * lv[1:2], axis=-1, keepdims=True))
           - jnp.exp(jnp.sum(lv[2:3] * lv[3:4], axis=-1, keepdims=True)) + lam_init)
    o = acc_ref[0:DIFF_V_DIM, :] * (1.0 / acc_ref[DIFF_V_DIM:DIFF_V_DIM + 1, :])
    d = o[:, 0:tq] - lam * o[:, tq:2 * tq]
    ms = jnp.mean(d * d, axis=0, keepdims=True)
    d = d * lax.rsqrt(ms + EPS) * g_ref[...] * (1.0 - lam_init)
    o_ref[...] = d.T.astype(BF16)


def _diff_attention(qbt, kb, vbt, lam_vec, subln_g, *, batch, n_q, q_row0, k_tile0, k_tiles,
                    tiles_per_batch, q_tile, lam_init):
    q_tiles = n_q // q_tile
    q_blk0 = q_row0 // q_tile
    q_sub = q_tile // ROW_TILE
    tiles_per_chunk = DIFF_CHUNK_TILES if k_tiles % DIFF_CHUNK_TILES == 0 else 1
    n_chunks = k_tiles // tiles_per_chunk
    tk = tiles_per_chunk * DIFF_K_TILE
    if k_tile0 == 0:
        k_spec = _resident((k_tiles * DIFF_K_TILE, LANES), lambda b, h, i: (b, h))
    else:
        k_spec = _resident((k_tiles * DIFF_K_TILE, LANES), lambda b, h, i: (b * tiles_per_batch + k_tile0, h))
    return pl.pallas_call(
        functools.partial(_diff_kernel, n_chunks=n_chunks, tiles_per_chunk=tiles_per_chunk, lam_init=lam_init),
        grid=(batch, DIFF_HEADS, q_tiles),
        in_specs=[pl.BlockSpec((q_sub, LANES, ROW_TILE), lambda b, h, i: (q_blk0 + b * q_tiles + i, h, 0)),
                  k_spec,
                  _resident((k_tiles, DIFF_V_ROWS, DIFF_K_TILE),
                            lambda b, h, i: ((b * tiles_per_batch + k_tile0) // k_tiles, h, 0)),
                  _resident((4, HEAD_DIM), lambda b, h, i: (0, 0)),
                  _resident((DIFF_V_DIM, 1), lambda b, h, i: (0, 0))],
        out_specs=pl.BlockSpec((q_tile, LANES), lambda b, h, i: (b * q_tiles + i, h)),
        out_shape=jax.ShapeDtypeStruct((batch * n_q, HALF_WIDTH), BF16),
        scratch_shapes=[pltpu.VMEM((LANES, 2 * q_tile), BF16),
                        pltpu.VMEM((tk, 2 * q_tile), F32),
                        pltpu.VMEM((1, 2 * q_tile), F32),
                        pltpu.VMEM((1, 2 * q_tile), F32),
                        pltpu.VMEM((DIFF_V_ROWS, 2 * q_tile), F32)],
        compiler_params=_cparams(3),
        name="differential_attention",
    )(qbt, kb, vbt, lam_vec, subln_g.reshape(DIFF_V_DIM, 1))


def _dense_kernel(q_ref, k_ref, v_ref, o_ref):
    tq = q_ref.shape[0]
    low = lax.broadcasted_iota(jnp.int32, (tq, LANES), 1) < HEAD_DIM
    for p in range(NA_HEADS // 2):
        sl = slice(p * LANES, (p + 1) * LANES)
        q, k, v = q_ref[:, sl], k_ref[:, sl], v_ref[:, sl]
        outs = []
        for e in range(2):
            qm = jnp.where(low if e == 0 else jnp.logical_not(low), q, jnp.zeros_like(q))
            s = lax.dot_general(qm, k, NT_DIMS, preferred_element_type=F32)
            pr = jnp.exp2(s - jnp.max(s, axis=-1, keepdims=True))
            o = jnp.dot(pr.astype(BF16), v, preferred_element_type=F32)
            outs.append(o * (1.0 / jnp.sum(pr, axis=-1, keepdims=True)))
        o_ref[:, sl] = jnp.where(low, outs[0], outs[1]).astype(BF16)


def _dense_attention(qa, ka, va, *, batch, q_row0, k_tile0, tiles_per_batch):
    q_blk0 = q_row0 // ROW_TILE
    kv_spec = pl.BlockSpec((ROW_TILE, HALF_WIDTH), lambda b: (b * tiles_per_batch + k_tile0, 0))
    return pl.pallas_call(
        _dense_kernel,
        grid=(batch,),
        in_specs=[pl.BlockSpec((ROW_TILE, HALF_WIDTH), lambda b: (q_blk0 + b, 0)), kv_spec, kv_spec],
        out_specs=pl.BlockSpec((ROW_TILE, HALF_WIDTH), lambda b: (b, 0)),
        out_shape=jax.ShapeDtypeStruct((batch * ROW_TILE, HALF_WIDTH), BF16),
        compiler_params=_cparams(1),
        name="context_attention",
    )(qa, ka, va)


def _conv_kernel(x_ref, xp_ref, xs_ref, mod_ref, g_ref, w_in_ref, cw_ref, w_out_ref, o_ref, *, tiles_per_seq):
    i = pl.program_id(0)
    tile, d = x_ref.shape
    halo = xp_ref.shape[0]
    x = x_ref[...]
    x_ext = jnp.concatenate([xp_ref[...], x, xs_ref[...]], axis=0)
    xn = _modulate(x_ext, g_ref[...], mod_ref[0, 3:4, :], mod_ref[0, 4:5, :]).astype(BF16)
    cg = jnp.dot(xn, w_in_ref[:, d:2 * d], preferred_element_type=F32)
    hh = jnp.dot(xn, w_in_ref[:, 2 * d:3 * d], preferred_element_type=F32)
    u = cg * hh
    n_ext = tile + 2 * halo
    ridx = lax.broadcasted_iota(jnp.int32, (n_ext, 1), 0)
    first = (i % tiles_per_seq) == 0
    last = (i % tiles_per_seq) == tiles_per_seq - 1
    u = jnp.where(jnp.logical_or(jnp.logical_and(first, ridx < halo),
                                 jnp.logical_and(last, ridx >= halo + tile)), 0.0, u)
    cw = cw_ref[...]
    u_prev = pltpu.roll(u, 1, 0)[halo:halo + tile]
    u_next = pltpu.roll(u, n_ext - 1, 0)[halo:halo + tile]
    v = u_prev * cw[0:1] + u[halo:halo + tile] * cw[1:2] + u_next * cw[2:3]
    bg = jnp.dot(xn[halo:halo + tile], w_in_ref[:, 0:d], preferred_element_type=F32)
    y = jnp.dot((bg * v).astype(BF16), w_out_ref[...], preferred_element_type=F32)
    o_ref[...] = x + mod_ref[0, 5:6, :] * y


def _conv_mixer(x, mods, group_of_tile, g, w_in, conv_w, w_out, *, tile, seq):
    rows, d = x.shape
    halo = 8
    tiles_per_seq = seq // tile
    hb = tile // halo
    n_hblocks = rows // halo
    return pl.pallas_call(
        functools.partial(_conv_kernel, tiles_per_seq=tiles_per_seq),
        grid=(rows // tile,),
        in_specs=[pl.BlockSpec((tile, d), lambda i: (i, 0)),
                  pl.BlockSpec((halo, d), lambda i: (jnp.maximum(i * hb - 1, 0), 0)),
                  pl.BlockSpec((halo, d), lambda i: (jnp.minimum((i + 1) * hb, n_hblocks - 1), 0)),
                  pl.BlockSpec((1, N_MOD, d), lambda i: (group_of_tile(i), 0, 0)),
                  _resident((1, d), lambda i: (0, 0)),
                  _resident((d, 3 * d), lambda i: (0, 0)),
                  _resident((CONV_WIDTH, d), lambda i: (0, 0)),
                  _resident((d, d), lambda i: (0, 0))],
        out_specs=pl.BlockSpec((tile, d), lambda i: (i, 0)),
        out_shape=jax.ShapeDtypeStruct((rows, d), F32),
        compiler_params=_cparams(1),
        name="conv_mixer",
    )(x, x, x, mods, g.reshape(1, d), w_in, conv_w, w_out)


def kernel(x, c, ctx, c_ctx, mod_w, mod_b, norm_g, ffn_w_gate, ffn_w_up, ffn_w_down, attn_w_in, attn_w_out,
           na_rpb, diff_lambda, diff_subln_g, conv_w_in, conv_w_out, conv_w, final_g):
    batch, n_lat, d = x.shape
    n_ctx = ctx.shape[1]
    depth = mod_w.shape[0]
    assert n_ctx == ROW_TILE and n_lat % LAT_TILE == 0 and depth == 2 and batch + 1 <= 8
    tiles_per_batch = (n_lat + n_ctx) // ROW_TILE
    lat_tiles = n_lat // ROW_TILE
    bf = lambda a: a.astype(BF16)

    cvec = jnp.concatenate([c, c_ctx[None, :], jnp.zeros((8 - batch - 1, d), F32)], axis=0)
    mods_all = _mod_vectors(cvec, mod_w, mod_b).reshape(depth, 8, N_MOD, d)
    ffn_w = (ffn_w_gate, ffn_w_up, ffn_w_down)

    lat_group = lambda i: i // (n_lat // LAT_TILE)
    ctx_group = lambda i: batch

    mods = mods_all[0]
    x_lat = _ffn(x.reshape(batch * n_lat, d), mods, norm_g[0, 0], ffn_w, 0, 0, tile=LAT_TILE, mod_base=0,
                 group_of_tile=lat_group)
    x_ctx = _ffn(ctx.reshape(batch * n_ctx, d), mods, norm_g[0, 0], ffn_w, 0, 0, tile=ROW_TILE, mod_base=0,
                 group_of_tile=ctx_group)
    qa, ka, va, qbt, kb, vbt = _attn_in(x_lat, x_ctx, mods, norm_g[0, 1], bf(attn_w_in[0]),
                                        _rope_tables(n_lat), batch=batch, tiles_per_batch=tiles_per_batch)
    lam_init = 0.8 - 0.6 * math.exp(-0.3 * 0)
    a_lat = _na_attention(qa, ka, va, _na_bias_tables(na_rpb[0], n_lat // GRID_W),
                          batch=batch, n_lat=n_lat, n_ctx=n_ctx)
    d_lat = _diff_attention(qbt, kb, vbt, diff_lambda[0], diff_subln_g[0], batch=batch, n_q=n_lat, q_row0=0,
                            k_tile0=0, k_tiles=tiles_per_batch, tiles_per_batch=tiles_per_batch,
                            q_tile=DIFF_Q_TILE, lam_init=lam_init)
    w_out = bf(attn_w_out[0])
    x_lat = _ffn(x_lat, mods, norm_g[0, 2], ffn_w, 0, 1, tile=LAT_TILE, mod_base=6, group_of_tile=lat_group,
                 attn=(a_lat, d_lat, w_out))

    a_ctx = _dense_attention(qa, ka, va, batch=batch, q_row0=batch * n_lat, k_tile0=lat_tiles,
                             tiles_per_batch=tiles_per_batch)
    d_ctx = _diff_attention(qbt, kb, vbt, diff_lambda[0], diff_subln_g[0], batch=batch, n_q=n_ctx,
                            q_row0=batch * n_lat, k_tile0=lat_tiles, k_tiles=1,
                            tiles_per_batch=tiles_per_batch, q_tile=ROW_TILE, lam_init=lam_init)
    x_ctx = _ffn(x_ctx, mods, norm_g[0, 2], ffn_w, 0, 1, tile=ROW_TILE, mod_base=6, group_of_tile=ctx_group,
                 attn=(a_ctx, d_ctx, w_out))
    del x_ctx

    mods = mods_all[1]
    x_lat = _ffn(x_lat, mods, norm_g[1, 0], ffn_w, 1, 0, tile=LAT_TILE, mod_base=0, group_of_tile=lat_group)
    x_lat = _conv_mixer(x_lat, mods, lat_group, norm_g[1, 1], bf(conv_w_in[0]), conv_w[0], bf(conv_w_out[0]),
                        tile=LAT_TILE, seq=n_lat)
    x_lat = _ffn(x_lat, mods, norm_g[1, 2], ffn_w, 1, 1, tile=LAT_TILE, mod_base=6, group_of_tile=lat_group,
                 final_g=final_g)
    return x_lat.reshape(batch, n_lat, d)
```

```python
import functools
import math

import jax
import jax.numpy as jnp
import numpy as np
from jax import lax
from jax.experimental import pallas as pl
from jax.experimental.pallas import tpu as pltpu

F32 = jnp.float32
BF16 = jnp.bfloat16

D_MODEL = 1024
GRID_W = 64
HEAD_DIM = 64
NA_HEADS = 8
NA_WIN_ROWS = 8
NA_WIN_COLS = 16
DIFF_HEADS = 4
DIFF_V_DIM = 128
DIFF_V_ROWS = 144
LOG2_E = math.log2(math.e)
HALF_WIDTH = 512
ATTN_IN_WIDTH = 6 * HALF_WIDTH
CONV_WIDTH = 3
D_FF = 2816
ROPE_THETA = 10000.0
N_MOD = 9
EPS = 1e-6
NEG_INF = -1e30

LANES = 128
ROW_TILE = 256
LAT_TILE = 512
FFN_TILE = 512
NA_Q_ROWS = 4
NA_BAND_ROWS = 12
DIFF_Q_TILE = 512
DIFF_K_TILE = 256
DIFF_CHUNK_TILES = 11
DIFF_UNROLL = 1
FF_CHUNKS = ((0, 1024), (1024, 1024), (2048, 768))
FFN_ROW_BLOCK = 256
FF_CAST_CHUNK = 256
VMEM_LIMIT = 56 * 1024 * 1024

NT_DIMS = (((1,), (1,)), ((), ()))


def _cparams(n_axes):
    return pltpu.CompilerParams(dimension_semantics=("arbitrary",) * n_axes,
                                vmem_limit_bytes=VMEM_LIMIT)


def _resident(shape, index_map):
    return pl.BlockSpec(shape, index_map, pipeline_mode=pl.Buffered(1))


def _modulate(x, g, shift, scale):
    ms = jnp.mean(x * x, axis=-1, keepdims=True)
    return x * lax.rsqrt(ms + EPS) * g * (1.0 + scale) + shift


def _silu(x):
    return x * (1.0 / (1.0 + jnp.exp(-x)))


def _mod_kernel(c_ref, w_ref, b_ref, o_ref):
    o_ref[0] = jnp.dot(_silu(c_ref[...]), w_ref[0], preferred_element_type=F32) + b_ref[0]


def _mod_vectors(cvec, mod_w, mod_b):
    depth, d, n = mod_w.shape
    bn = 1536
    return pl.pallas_call(
        _mod_kernel,
        grid=(depth, n // bn),
        in_specs=[pl.BlockSpec((8, d), lambda i, j: (0, 0)),
                  pl.BlockSpec((1, d, bn), lambda i, j: (i, 0, j)),
                  pl.BlockSpec((1, 1, bn), lambda i, j: (i, 0, j))],
        out_specs=pl.BlockSpec((1, 8, bn), lambda i, j: (i, 0, j)),
        out_shape=jax.ShapeDtypeStruct((depth, 8, n), F32),
        compiler_params=_cparams(2),
        name="adaln_vectors",
    )(cvec, mod_w, mod_b.reshape(depth, 1, n))


def _ffn_kernel(*refs, mod_base, attn, final, n_cast):
    refs = list(refs)
    wg_ref, wu_ref, wd_ref = refs[-3:]
    o_ref = refs[-4]
    x_ref = refs.pop(0)
    if attn:
        a_ref, d_ref, wo_ref = refs.pop(0), refs.pop(0), refs.pop(0)
    mod_ref, g_ref, wg32_ref, wu32_ref, wd32_ref = refs[:5]
    step = pl.program_id(0)

    for c in range(n_cast):
        @pl.when(step == c)
        def _(c=c):
            sl = slice(c * FF_CAST_CHUNK, (c + 1) * FF_CAST_CHUNK)
            wg_ref[:, sl] = wg32_ref[...].astype(BF16)
            wu_ref[:, sl] = wu32_ref[...].astype(BF16)
            wd_ref[sl, :] = wd32_ref[...].astype(BF16)

    @pl.when(step >= n_cast)
    def _():
        shift = mod_ref[0, mod_base:mod_base + 1, :]
        scale = mod_ref[0, mod_base + 1:mod_base + 2, :]
        gate = mod_ref[0, mod_base + 2:mod_base + 3, :]
        blocks = [slice(r, r + FFN_ROW_BLOCK) for r in range(0, x_ref.shape[0], FFN_ROW_BLOCK)]
        xs, xns = [], []
        for rs in blocks:
            x = x_ref[rs, :]
            if attn:
                y = (jnp.dot(a_ref[rs, :], wo_ref[0:HALF_WIDTH, :], preferred_element_type=F32)
                     + jnp.dot(d_ref[rs, :], wo_ref[HALF_WIDTH:2 * HALF_WIDTH, :], preferred_element_type=F32))
                x = x + mod_ref[0, 5:6, :] * y
            xs.append(x)
            xns.append(_modulate(x, g_ref[...], shift, scale).astype(BF16))
        for rs, x, xn in zip(blocks, xs, xns):
            y = None
            for lo, width in FF_CHUNKS:
                hg = jnp.dot(xn, wg_ref[:, lo:lo + width], preferred_element_type=F32)
                hu = jnp.dot(xn, wu_ref[:, lo:lo + width], preferred_element_type=F32)
                a = (_silu(hg) * hu).astype(BF16)
                part = jnp.dot(a, wd_ref[lo:lo + width, :], preferred_element_type=F32)
                y = part if y is None else y + part
            out = x + (0.5 * gate) * y
            if final:
                fg = refs[5][...]
                ms = jnp.mean(out * out, axis=-1, keepdims=True)
                out = out * lax.rsqrt(ms + EPS) * fg
            o_ref[rs, :] = out


def _ffn(x, mods, g, weights, layer, which, *, tile, mod_base, group_of_tile, attn=None, final_g=None):
    rows, d = x.shape
    n_tiles = rows // tile
    wg, wu, wd = weights
    f = wg.shape[-1]
    n_cast = f // FF_CAST_CHUNK
    row_tile = lambda s: jnp.maximum(s - n_cast, 0)
    slab = lambda s: jnp.minimum(s, n_cast - 1)
    in_specs = [pl.BlockSpec((tile, d), lambda s: (row_tile(s), 0))]
    args = [x]
    if attn is not None:
        a, dd, w_out = attn
        in_specs += [pl.BlockSpec((tile, HALF_WIDTH), lambda s: (row_tile(s), 0)),
                     pl.BlockSpec((tile, HALF_WIDTH), lambda s: (row_tile(s), 0)),
                     _resident((2 * HALF_WIDTH, d), lambda s: (0, 0))]
        args += [a, dd, w_out]
    in_specs += [pl.BlockSpec((1, N_MOD, d), lambda s: (group_of_tile(row_tile(s)), 0, 0)),
                 _resident((1, d), lambda s: (0, 0)),
                 pl.BlockSpec((None, None, d, FF_CAST_CHUNK), lambda s: (layer, which, 0, slab(s))),
                 pl.BlockSpec((None, None, d, FF_CAST_CHUNK), lambda s: (layer, which, 0, slab(s))),
                 pl.BlockSpec((None, None, FF_CAST_CHUNK, d), lambda s: (layer, which, slab(s), 0))]
    args += [mods, g.reshape(1, d), wg, wu, wd]
    if final_g is not None:
        in_specs.append(_resident((1, d), lambda s: (0, 0)))
        args.append(final_g.reshape(1, d))
    return pl.pallas_call(
        functools.partial(_ffn_kernel, mod_base=mod_base, attn=attn is not None, final=final_g is not None,
                          n_cast=n_cast),
        grid=(n_cast + n_tiles,),
        in_specs=in_specs,
        out_specs=pl.BlockSpec((tile, d), lambda s: (row_tile(s), 0)),
        out_shape=jax.ShapeDtypeStruct((rows, d), F32),
        scratch_shapes=[pltpu.VMEM((d, f), BF16), pltpu.VMEM((d, f), BF16), pltpu.VMEM((f, d), BF16)],
        compiler_params=_cparams(1),
        name="macaron_ffn",
    )(*args)


def _rope(z, cos, sin_next, sin_prev):
    return z * cos + pltpu.roll(z, LANES - 1, 1) * sin_next + pltpu.roll(z, 1, 1) * sin_prev


def _rope_tile_tables(row_ref, col_ref, is_ctx):
    grid_rows = ROW_TILE // GRID_W
    lane = lax.broadcasted_iota(jnp.int32, (ROW_TILE, LANES), 1)
    by_row = (lane % HEAD_DIM) // 2 < HEAD_DIM // 4
    tabs = []
    for k, identity in enumerate((1.0, 0.0, 0.0)):
        rows = row_ref[k, 0]
        row_part = jnp.concatenate([jnp.broadcast_to(rows[r:r + 1], (GRID_W, LANES)) for r in range(grid_rows)],
                                   axis=0)
        col_part = jnp.concatenate([col_ref[k]] * grid_rows, axis=0)
        tabs.append(jnp.where(is_ctx, identity, jnp.where(by_row, row_part, col_part)))
    return tabs


def _attn_in_kernel(x_ref, xc_ref, mod_ref, g_ref, w_ref, row_ref, col_ref,
                    qa_ref, ka_ref, va_ref, qbt_ref, kb_ref, vbt_ref, *, tiles_per_batch):
    is_ctx = pl.program_id(0) % tiles_per_batch == tiles_per_batch - 1
    x = jnp.where(is_ctx, xc_ref[...], x_ref[...])
    xn = _modulate(x, g_ref[...], mod_ref[0, 3:4, :], mod_ref[0, 4:5, :]).astype(BF16)
    w = HALF_WIDTH
    scale = HEAD_DIM ** -0.5 * LOG2_E
    qa_ref[...] = (jnp.dot(xn, w_ref[:, 0:w], preferred_element_type=F32) * scale).astype(BF16)
    ka_ref[...] = jnp.dot(xn, w_ref[:, w:2 * w], preferred_element_type=F32).astype(BF16)
    va_ref[...] = jnp.dot(xn, w_ref[:, 2 * w:3 * w], preferred_element_type=F32).astype(BF16)
    cos, sn, sp = _rope_tile_tables(row_ref, col_ref, is_ctx)
    qb = jnp.dot(xn, w_ref[:, 3 * w:4 * w], preferred_element_type=F32)
    kb = jnp.dot(xn, w_ref[:, 4 * w:5 * w], preferred_element_type=F32)
    for h in range(DIFF_HEADS):
        sl = slice(h * LANES, (h + 1) * LANES)
        qbt_ref[0, sl, :] = (_rope(qb[:, sl], cos, sn, sp) * scale).T.astype(BF16)
        kb_ref[:, sl] = _rope(kb[:, sl], cos, sn, sp).astype(BF16)
    vbt = jnp.dot(xn, w_ref[:, 5 * w:6 * w], preferred_element_type=F32).T.astype(BF16)
    pad_rows = DIFF_V_ROWS - DIFF_V_DIM
    ones_row = (lax.broadcasted_iota(jnp.int32, (pad_rows, vbt.shape[1]), 0) == 0).astype(BF16)
    for h in range(DIFF_HEADS):
        vbt_ref[0, h * DIFF_V_ROWS:h * DIFF_V_ROWS + DIFF_V_DIM, :] = vbt[h * DIFF_V_DIM:(h + 1) * DIFF_V_DIM]
        vbt_ref[0, h * DIFF_V_ROWS + DIFF_V_DIM:(h + 1) * DIFF_V_ROWS, :] = ones_row


def _rope_tables(n_lat):
    n_freq = HEAD_DIM // 4
    grid_rows = ROW_TILE // GRID_W
    lane = jnp.arange(LANES)
    pair = (lane % HEAD_DIM) // 2
    inv = ROPE_THETA ** (-(pair % n_freq).astype(F32) / n_freq)
    even = lane % 2 == 0

    def tables(pos):
        ang = pos.astype(F32)[:, None] * inv[None, :]
        sin = jnp.sin(ang)
        return jnp.stack([jnp.cos(ang), jnp.where(even, -sin, 0.0), jnp.where(even, 0.0, sin)])

    n_rows = n_lat // GRID_W
    row_tabs = tables(jnp.arange(n_rows)).reshape(3, n_rows // grid_rows, grid_rows, LANES)
    row_tabs = jnp.pad(row_tabs, ((0, 0), (0, 0), (0, 8 - grid_rows), (0, 0)))
    return row_tabs, tables(jnp.arange(GRID_W))


def _attn_in(x_lat, x_ctx, mods, g, w_in, tables, *, batch, tiles_per_batch):
    d = x_lat.shape[1]
    lat_tiles = tiles_per_batch - 1
    n_tiles = batch * tiles_per_batch
    rows = n_tiles * ROW_TILE

    def group(i):
        return jnp.where(i % tiles_per_batch == lat_tiles, batch, i // tiles_per_batch)

    def lat_first(i):
        b, j = i // tiles_per_batch, i % tiles_per_batch
        return jnp.where(j == lat_tiles, batch * lat_tiles + b, b * lat_tiles + j)

    half = jax.ShapeDtypeStruct((rows, HALF_WIDTH), BF16)
    q_spec = pl.BlockSpec((ROW_TILE, HALF_WIDTH), lambda i: (lat_first(i), 0))
    kv_spec = pl.BlockSpec((ROW_TILE, HALF_WIDTH), lambda i: (i, 0))
    def lat_tile(i):
        return (i // tiles_per_batch) * lat_tiles + jnp.minimum(i % tiles_per_batch, lat_tiles - 1)

    return pl.pallas_call(
        functools.partial(_attn_in_kernel, tiles_per_batch=tiles_per_batch),
        grid=(n_tiles,),
        in_specs=[pl.BlockSpec((ROW_TILE, d), lambda i: (lat_tile(i), 0)),
                  pl.BlockSpec((ROW_TILE, d), lambda i: (i // tiles_per_batch, 0)),
                  pl.BlockSpec((1, N_MOD, d), lambda i: (group(i), 0, 0)),
                  _resident((1, d), lambda i: (0, 0)),
                  _resident((d, ATTN_IN_WIDTH), lambda i: (0, 0)),
                  pl.BlockSpec((3, 1, 8, LANES), lambda i: (0, jnp.minimum(i % tiles_per_batch, lat_tiles - 1), 0, 0)),
                  _resident((3, GRID_W, LANES), lambda i: (0, 0, 0))],
        out_specs=[q_spec, kv_spec, kv_spec,
                   pl.BlockSpec((1, HALF_WIDTH, ROW_TILE), lambda i: (lat_first(i), 0, 0)), kv_spec,
                   pl.BlockSpec((1, DIFF_HEADS * DIFF_V_ROWS, ROW_TILE), lambda i: (i, 0, 0))],
        out_shape=[half, half, half,
                   jax.ShapeDtypeStruct((n_tiles, HALF_WIDTH, ROW_TILE), BF16), half,
                   jax.ShapeDtypeStruct((n_tiles, DIFF_HEADS * DIFF_V_ROWS, ROW_TILE), BF16)],
        compiler_params=_cparams(1),
        name="attn_in_proj",
    )(x_lat, x_ctx, mods, g.reshape(1, d), w_in, *tables)


def _na_bias_tables(rpb, rows):
    heads = rpb.shape[0]
    kr_win = min(NA_WIN_ROWS, rows)
    qc = np.arange(GRID_W)
    col_start = np.clip(qc - NA_WIN_COLS // 2, 0, GRID_W - NA_WIN_COLS)
    col_ok = (qc[None, :] >= col_start[:, None]) & (qc[None, :] < col_start[:, None] + NA_WIN_COLS)
    rpb = rpb.astype(F32) * LOG2_E
    n_dr, n_dc = 2 * NA_WIN_ROWS - 1, 2 * NA_WIN_COLS - 1
    left = GRID_W - NA_WIN_COLS
    u = jnp.pad(rpb, ((0, 0), (0, 0), (left, 2 * GRID_W - n_dc - left)), constant_values=NEG_INF)
    u = jnp.broadcast_to(u[:, :, None, :], (heads, n_dr, GRID_W, 2 * GRID_W)).reshape(heads, n_dr, -1)
    t = u[:, :, GRID_W - 1:GRID_W - 1 + GRID_W * (2 * GRID_W - 1)].reshape(heads, n_dr, GRID_W, 2 * GRID_W - 1)
    t = jnp.where(col_ok[None, None], t[..., :GRID_W], NEG_INF)
    neg = jnp.full((heads, GRID_W, GRID_W), NEG_INF, F32)
    out = []
    for r0, ws in ((0, 0), (NA_Q_ROWS, 0), (rows - NA_Q_ROWS, rows - NA_BAND_ROWS)):
        block_rows = []
        for i in range(NA_Q_ROWS):
            qr = r0 + i
            start = min(max(qr - kr_win // 2, 0), rows - kr_win)
            blocks = [t[:, kr - qr + NA_WIN_ROWS - 1] if start <= kr < start + kr_win else neg
                      for kr in range(ws, ws + NA_BAND_ROWS)]
            block_rows.append(jnp.concatenate(blocks, axis=-1))
        out.append(jnp.concatenate(block_rows, axis=1))
    return jnp.stack(out)


def _na_kernel(q_ref, k_ref, v_ref, bias_ref, o_ref, *, n_lat, n_ctx, rows):
    g = pl.program_id(1)
    ws = jnp.clip(NA_Q_ROWS * g - NA_WIN_ROWS // 2, 0, rows - NA_BAND_ROWS)
    start = pl.multiple_of(ws * GRID_W, GRID_W)
    band = NA_BAND_ROWS * GRID_W
    tq = NA_Q_ROWS * GRID_W
    low = lax.broadcasted_iota(jnp.int32, (tq, LANES), 1) < HEAD_DIM

    def scores(h):
        sl = slice(h // 2 * LANES, (h // 2 + 1) * LANES)
        q = q_ref[:, sl]
        qm = jnp.where(low if h % 2 == 0 else jnp.logical_not(low), q, jnp.zeros_like(q))
        sb = lax.dot_general(qm, k_ref[pl.ds(start, band), sl], NT_DIMS, preferred_element_type=F32)
        sc = lax.dot_general(qm, k_ref[n_lat:n_lat + n_ctx, sl], NT_DIMS, preferred_element_type=F32)
        return sb + bias_ref[0, h], sc

    def attend(h, sb, sc):
        sl = slice(h // 2 * LANES, (h // 2 + 1) * LANES)
        m = jnp.maximum(jnp.max(sb, axis=-1, keepdims=True), jnp.max(sc, axis=-1, keepdims=True))
        pb = jnp.exp2(sb - m)
        pc = jnp.exp2(sc - m)
        l = jnp.sum(pb, axis=-1, keepdims=True) + jnp.sum(pc, axis=-1, keepdims=True)
        o = (jnp.dot(pb.astype(BF16), v_ref[pl.ds(start, band), sl], preferred_element_type=F32)
             + jnp.dot(pc.astype(BF16), v_ref[n_lat:n_lat + n_ctx, sl], preferred_element_type=F32))
        return o * (1.0 / l)

    s_cur = scores(0)
    even_out = None
    for h in range(NA_HEADS):
        s_nxt = scores(h + 1) if h + 1 < NA_HEADS else None
        o = attend(h, *s_cur)
        if h % 2 == 0:
            even_out = o
        else:
            sl = slice(h // 2 * LANES, (h // 2 + 1) * LANES)
            o_ref[:, sl] = jnp.where(low, even_out, o).astype(BF16)
        s_cur = s_nxt


def _na_attention(qa, ka, va, bias, *, batch, n_lat, n_ctx):
    rows = n_lat // GRID_W
    groups = rows // NA_Q_ROWS
    tq = NA_Q_ROWS * GRID_W
    per_batch = n_lat + n_ctx

    def variant(g):
        return jnp.where(g == 0, 0, jnp.where(g == groups - 1, 2, 1))

    kv_spec = _resident((per_batch, HALF_WIDTH), lambda b, g: (b, 0))
    return pl.pallas_call(
        functools.partial(_na_kernel, n_lat=n_lat, n_ctx=n_ctx, rows=rows),
        grid=(batch, groups),
        in_specs=[pl.BlockSpec((tq, HALF_WIDTH), lambda b, g: (b * groups + g, 0)),
                  kv_spec, kv_spec,
                  pl.BlockSpec((1, NA_HEADS, tq, NA_BAND_ROWS * GRID_W), lambda b, g: (variant(g), 0, 0, 0))],
        out_specs=pl.BlockSpec((tq, HALF_WIDTH), lambda b, g: (b * groups + g, 0)),
        out_shape=jax.ShapeDtypeStruct((batch * n_lat, HALF_WIDTH), BF16),
        compiler_params=_cparams(2),
        name="neighbourhood_attention",
    )(qa, ka, va, bias)


def _diff_kernel(q_ref, qn_ref, k_ref, vt_ref, lam_ref, g_ref, o_ref, qc_ref, s_ref, m_ref, alpha_ref, acc_ref,
                 *, n_chunks, tiles_per_chunk, lam_init):
    tq = q_ref.shape[0] * q_ref.shape[2]
    tk = tiles_per_chunk * DIFF_K_TILE

    def load_q(ref):
        qt = jnp.concatenate([ref[j] for j in range(ref.shape[0])], axis=1)
        low = lax.broadcasted_iota(jnp.int32, qt.shape, 0) < HEAD_DIM
        zero = jnp.zeros_like(qt)
        qc_ref[:, 0:tq] = jnp.where(low, qt, zero)
        qc_ref[:, tq:2 * tq] = jnp.where(low, zero, qt)

    def tile_scores(c, t):
        row0 = pl.multiple_of(c * tk + t * DIFF_K_TILE, DIFF_K_TILE)
        s = jnp.dot(k_ref[pl.ds(row0, DIFF_K_TILE), :], qc_ref[...], preferred_element_type=F32)
        s_ref[t * DIFF_K_TILE:(t + 1) * DIFF_K_TILE, :] = s
        return jnp.max(s, axis=0, keepdims=True)

    def finish_max(cmax, fresh):
        if fresh:
            alpha_ref[...] = jnp.zeros(alpha_ref.shape, F32)
            m_ref[...] = cmax
        else:
            m_old = m_ref[...]
            m_new = jnp.maximum(m_old, cmax)
            alpha_ref[...] = jnp.exp2(m_old - m_new)
            m_ref[...] = m_new

    def step(c, next_chunk, fresh):
        m = m_ref[...]
        alpha = alpha_ref[...]
        cmax = None
        for t in range(tiles_per_chunk):
            p = jnp.exp2(s_ref[t * DIFF_K_TILE:(t + 1) * DIFF_K_TILE, :] - m).astype(BF16)
            tmax = tile_scores(next_chunk, t)
            cmax = tmax if cmax is None else jnp.maximum(cmax, tmax)
            pv = jnp.dot(vt_ref[c * tiles_per_chunk + t], p, preferred_element_type=F32)
            acc_ref[...] = (alpha * acc_ref[...] if t == 0 else acc_ref[...]) + pv
        finish_max(cmax, fresh)

    load_q(q_ref)
    acc_ref[...] = jnp.zeros(acc_ref.shape, F32)

    @pl.when(pl.program_id(2) == 0)
    def _():
        cmax = None
        for t in range(tiles_per_chunk):
            tmax = tile_scores(0, t)
            cmax = tmax if cmax is None else jnp.maximum(cmax, tmax)
        finish_max(cmax, True)

    def body(c, carry):
        step(c, c + 1, False)
        return carry

    lax.fori_loop(0, n_chunks - 1, body, 0, unroll=DIFF_UNROLL)
    load_q(qn_ref)
    step(n_chunks - 1, 0, True)
    lv = lam_ref[...]
    lam = (jnp.exp(jnp.sum(lv[0:1] * lv[1:2], axis=-1, keepdims=True))
           - jnp.exp(jnp.sum(lv[2:3] * lv[3:4], axis=-1, keepdims=True)) + lam_init)
    o = acc_ref[0:DIFF_V_DIM, :] * (1.0 / acc_ref[DIFF_V_DIM:DIFF_V_DIM + 1, :])
    d = o[:, 0:tq] - lam * o[:, tq:2 * tq]
    ms = jnp.mean(d * d, axis=0, keepdims=True)
    d = d * lax.rsqrt(ms + EPS) * g_ref[...] * (1.0 - lam_init)
    o_ref[...] = d.T.astype(BF16)


def _diff_attention(qbt, kb, vbt, lam_vec, subln_g, *, batch, n_q, q_row0, k_tile0, k_tiles,
                    tiles_per_batch, q_tile, lam_init):
    q_tiles = n_q // q_tile
    q_blk0 = q_row0 // q_tile
    q_sub = q_tile // ROW_TILE
    tiles_per_chunk = DIFF_CHUNK_TILES if k_tiles % DIFF_CHUNK_TILES == 0 else 1
    n_chunks = k_tiles // tiles_per_chunk
    tk = tiles_per_chunk * DIFF_K_TILE
    if k_tile0 == 0:
        k_spec = _resident((k_tiles * DIFF_K_TILE, LANES), lambda b, h, i: (b, h))
    else:
        k_spec = _resident((k_tiles * DIFF_K_TILE, LANES), lambda b, h, i: (b * tiles_per_batch + k_tile0, h))
    return pl.pallas_call(
        functools.partial(_diff_kernel, n_chunks=n_chunks, tiles_per_chunk=tiles_per_chunk, lam_init=lam_init),
        grid=(batch, DIFF_HEADS, q_tiles),
        in_specs=[pl.BlockSpec((q_sub, LANES, ROW_TILE), lambda b, h, i: (q_blk0 + b * q_tiles + i, h, 0)),
                  pl.BlockSpec((q_sub, LANES, ROW_TILE),
                               lambda b, h, i: (q_blk0 + b * q_tiles + jnp.minimum(i + 1, q_tiles - 1), h, 0)),
                  k_spec,
                  _resident((k_tiles, DIFF_V_ROWS, DIFF_K_TILE),
                            lambda b, h, i: ((b * tiles_per_batch + k_tile0) // k_tiles, h, 0)),
                  _resident((4, HEAD_DIM), lambda b, h, i: (0, 0)),
                  _resident((DIFF_V_DIM, 1), lambda b, h, i: (0, 0))],
        out_specs=pl.BlockSpec((q_tile, LANES), lambda b, h, i: (b * q_tiles + i, h)),
        out_shape=jax.ShapeDtypeStruct((batch * n_q, HALF_WIDTH), BF16),
        scratch_shapes=[pltpu.VMEM((LANES, 2 * q_tile), BF16),
                        pltpu.VMEM((tk, 2 * q_tile), F32),
                        pltpu.VMEM((1, 2 * q_tile), F32),
                        pltpu.VMEM((1, 2 * q_tile), F32),
                        pltpu.VMEM((DIFF_V_ROWS, 2 * q_tile), F32)],
        compiler_params=_cparams(3),
        name="differential_attention",
    )(qbt, qbt, kb, vbt, lam_vec, subln_g.reshape(DIFF_V_DIM, 1))


def _dense_kernel(q_ref, k_ref, v_ref, o_ref):
    tq = q_ref.shape[0]
    low = lax.broadcasted_iota(jnp.int32, (tq, LANES), 1) < HEAD_DIM
    for p in range(NA_HEADS // 2):
        sl = slice(p * LANES, (p + 1) * LANES)
        q, k, v = q_ref[:, sl], k_ref[:, sl], v_ref[:, sl]
        outs = []
        for e in range(2):
            qm = jnp.where(low if e == 0 else jnp.logical_not(low), q, jnp.zeros_like(q))
            s = lax.dot_general(qm, k, NT_DIMS, preferred_element_type=F32)
            pr = jnp.exp2(s - jnp.max(s, axis=-1, keepdims=True))
            o = jnp.dot(pr.astype(BF16), v, preferred_element_type=F32)
            outs.append(o * (1.0 / jnp.sum(pr, axis=-1, keepdims=True)))
        o_ref[:, sl] = jnp.where(low, outs[0], outs[1]).astype(BF16)


def _dense_attention(qa, ka, va, *, batch, q_row0, k_tile0, tiles_per_batch):
    q_blk0 = q_row0 // ROW_TILE
    kv_spec = pl.BlockSpec((ROW_TILE, HALF_WIDTH), lambda b: (b * tiles_per_batch + k_tile0, 0))
    return pl.pallas_call(
        _dense_kernel,
        grid=(batch,),
        in_specs=[pl.BlockSpec((ROW_TILE, HALF_WIDTH), lambda b: (q_blk0 + b, 0)), kv_spec, kv_spec],
        out_specs=pl.BlockSpec((ROW_TILE, HALF_WIDTH), lambda b: (b, 0)),
        out_shape=jax.ShapeDtypeStruct((batch * ROW_TILE, HALF_WIDTH), BF16),
        compiler_params=_cparams(1),
        name="context_attention",
    )(qa, ka, va)


def _conv_kernel(x_ref, xp_ref, xs_ref, mod_ref, g_ref, w_in_ref, cw_ref, w_out_ref, o_ref, *, tiles_per_seq):
    i = pl.program_id(0)
    tile, d = x_ref.shape
    halo = xp_ref.shape[0]
    x = x_ref[...]
    x_ext = jnp.concatenate([xp_ref[...], x, xs_ref[...]], axis=0)
    xn = _modulate(x_ext, g_ref[...], mod_ref[0, 3:4, :], mod_ref[0, 4:5, :]).astype(BF16)
    cg = jnp.dot(xn, w_in_ref[:, d:2 * d], preferred_element_type=F32)
    hh = jnp.dot(xn, w_in_ref[:, 2 * d:3 * d], preferred_element_type=F32)
    u = cg * hh
    n_ext = tile + 2 * halo
    ridx = lax.broadcasted_iota(jnp.int32, (n_ext, 1), 0)
    first = (i % tiles_per_seq) == 0
    last = (i % tiles_per_seq) == tiles_per_seq - 1
    u = jnp.where(jnp.logical_or(jnp.logical_and(first, ridx < halo),
                                 jnp.logical_and(last, ridx >= halo + tile)), 0.0, u)
    cw = cw_ref[...]
    u_prev = pltpu.roll(u, 1, 0)[halo:halo + tile]
    u_next = pltpu.roll(u, n_ext - 1, 0)[halo:halo + tile]
    v = u_prev * cw[0:1] + u[halo:halo + tile] * cw[1:2] + u_next * cw[2:3]
    bg = jnp.dot(xn[halo:halo + tile], w_in_ref[:, 0:d], preferred_element_type=F32)
    y = jnp.dot((bg * v).astype(BF16), w_out_ref[...], preferred_element_type=F32)
    o_ref[...] = x + mod_ref[0, 5:6, :] * y


def _conv_mixer(x, mods, group_of_tile, g, w_in, conv_w, w_out, *, tile, seq):
    rows, d = x.shape
    halo = 8
    tiles_per_seq = seq // tile
    hb = tile // halo
    n_hblocks = rows // halo
    return pl.pallas_call(
        functools.partial(_conv_kernel, tiles_per_seq=tiles_per_seq),
        grid=(rows // tile,),
        in_specs=[pl.BlockSpec((tile, d), lambda i: (i, 0)),
                  pl.BlockSpec((halo, d), lambda i: (jnp.maximum(i * hb - 1, 0), 0)),
                  pl.BlockSpec((halo, d), lambda i: (jnp.minimum((i + 1) * hb, n_hblocks - 1), 0)),
                  pl.BlockSpec((1, N_MOD, d), lambda i: (group_of_tile(i), 0, 0)),
                  _resident((1, d), lambda i: (0, 0)),
                  _resident((d, 3 * d), lambda i: (0, 0)),
                  _resident((CONV_WIDTH, d), lambda i: (0, 0)),
                  _resident((d, d), lambda i: (0, 0))],
        out_specs=pl.BlockSpec((tile, d), lambda i: (i, 0)),
        out_shape=jax.ShapeDtypeStruct((rows, d), F32),
        compiler_params=_cparams(1),
        name="conv_mixer",
    )(x, x, x, mods, g.reshape(1, d), w_in, conv_w, w_out)


def kernel(x, c, ctx, c_ctx, mod_w, mod_b, norm_g, ffn_w_gate, ffn_w_up, ffn_w_down, attn_w_in, attn_w_out,
           na_rpb, diff_lambda, diff_subln_g, conv_w_in, conv_w_out, conv_w, final_g):
    batch, n_lat, d = x.shape
    n_ctx = ctx.shape[1]
    depth = mod_w.shape[0]
    assert n_ctx == ROW_TILE and n_lat % FFN_TILE == 0 and depth == 2 and batch + 1 <= 8
    tiles_per_batch = (n_lat + n_ctx) // ROW_TILE
    lat_tiles = n_lat // ROW_TILE
    bf = lambda a: a.astype(BF16)

    cvec = jnp.concatenate([c, c_ctx[None, :], jnp.zeros((8 - batch - 1, d), F32)], axis=0)
    mods_all = _mod_vectors(cvec, mod_w, mod_b).reshape(depth, 8, N_MOD, d)
    ffn_w = (ffn_w_gate, ffn_w_up, ffn_w_down)

    ffn_group = lambda i: i // (n_lat // FFN_TILE)
    conv_group = lambda i: i // (n_lat // LAT_TILE)
    ctx_group = lambda i: batch

    mods = mods_all[0]
    x_lat = _ffn(x.reshape(batch * n_lat, d), mods, norm_g[0, 0], ffn_w, 0, 0, tile=FFN_TILE, mod_base=0,
                 group_of_tile=ffn_group)
    x_ctx = _ffn(ctx.reshape(batch * n_ctx, d), mods, norm_g[0, 0], ffn_w, 0, 0, tile=ROW_TILE, mod_base=0,
                 group_of_tile=ctx_group)
    qa, ka, va, qbt, kb, vbt = _attn_in(x_lat, x_ctx, mods, norm_g[0, 1], bf(attn_w_in[0]),
                                        _rope_tables(n_lat), batch=batch, tiles_per_batch=tiles_per_batch)
    lam_init = 0.8 - 0.6 * math.exp(-0.3 * 0)
    a_lat = _na_attention(qa, ka, va, _na_bias_tables(na_rpb[0], n_lat // GRID_W),
                          batch=batch, n_lat=n_lat, n_ctx=n_ctx)
    d_lat = _diff_attention(qbt, kb, vbt, diff_lambda[0], diff_subln_g[0], batch=batch, n_q=n_lat, q_row0=0,
                            k_tile0=0, k_tiles=tiles_per_batch, tiles_per_batch=tiles_per_batch,
                            q_tile=DIFF_Q_TILE, lam_init=lam_init)
    w_out = bf(attn_w_out[0])
    x_lat = _ffn(x_lat, mods, norm_g[0, 2], ffn_w, 0, 1, tile=FFN_TILE, mod_base=6, group_of_tile=ffn_group,
                 attn=(a_lat, d_lat, w_out))

    a_ctx = _dense_attention(qa, ka, va, batch=batch, q_row0=batch * n_lat, k_tile0=lat_tiles,
                             tiles_per_batch=tiles_per_batch)
    d_ctx = _diff_attention(qbt, kb, vbt, diff_lambda[0], diff_subln_g[0], batch=batch, n_q=n_ctx,
                            q_row0=batch * n_lat, k_tile0=lat_tiles, k_tiles=1,
                            tiles_per_batch=tiles_per_batch, q_tile=ROW_TILE, lam_init=lam_init)
    x_ctx = _ffn(x_ctx, mods, norm_g[0, 2], ffn_w, 0, 1, tile=ROW_TILE, mod_base=6, group_of_tile=ctx_group,
                 attn=(a_ctx, d_ctx, w_out))
    del x_ctx

    mods = mods_all[1]
    x_lat = _ffn(x_lat, mods, norm_g[1, 0], ffn_w, 1, 0, tile=FFN_TILE, mod_base=0, group_of_tile=ffn_group)
    x_lat = _conv_mixer(x_lat, mods, conv_group, norm_g[1, 1], bf(conv_w_in[0]), conv_w[0], bf(conv_w_out[0]),
                        tile=LAT_TILE, seq=n_lat)
    x_lat = _ffn(x_lat, mods, norm_g[1, 2], ffn_w, 1, 1, tile=FFN_TILE, mod_base=6, group_of_tile=ffn_group,
                 final_g=final_g)
    return x_lat.reshape(batch, n_lat, d)
```

```python
import functools
import math

import jax
import jax.numpy as jnp
import numpy as np
from jax import lax
from jax.experimental import pallas as pl
from jax.experimental.pallas import tpu as pltpu

F32 = jnp.float32
BF16 = jnp.bfloat16

D_MODEL = 1024
GRID_W = 64
HEAD_DIM = 64
NA_HEADS = 8
NA_WIN_ROWS = 8
NA_WIN_COLS = 16
DIFF_HEADS = 4
DIFF_V_DIM = 128
DIFF_V_ROWS = 144
LOG2_E = math.log2(math.e)
HALF_WIDTH = 512
ATTN_IN_WIDTH = 6 * HALF_WIDTH
CONV_WIDTH = 3
D_FF = 2816
ROPE_THETA = 10000.0
N_MOD = 9
EPS = 1e-6
NEG_INF = -1e30

LANES = 128
ROW_TILE = 256
LAT_TILE = 512
FFN_TILE = 512
NA_Q_ROWS = 4
NA_BAND_ROWS = 12
DIFF_Q_TILE = 512
DIFF_K_TILE = 256
DIFF_CHUNK_TILES = 11
DIFF_UNROLL = 1
FF_CHUNKS = ((0, 1024), (1024, 1024), (2048, 768))
FFN_ROW_BLOCK = 256
FF_CAST_CHUNK = 256
VMEM_LIMIT = 56 * 1024 * 1024

NT_DIMS = (((1,), (1,)), ((), ()))


def _cparams(n_axes):
    return pltpu.CompilerParams(dimension_semantics=("arbitrary",) * n_axes,
                                vmem_limit_bytes=VMEM_LIMIT)


def _resident(shape, index_map):
    return pl.BlockSpec(shape, index_map, pipeline_mode=pl.Buffered(1))


def _modulate(x, g, shift, scale):
    ms = jnp.mean(x * x, axis=-1, keepdims=True)
    return x * lax.rsqrt(ms + EPS) * g * (1.0 + scale) + shift


def _silu(x):
    return x * (1.0 / (1.0 + jnp.exp(-x)))


def _mod_kernel(c_ref, w_ref, b_ref, o_ref):
    o_ref[0] = jnp.dot(_silu(c_ref[...]), w_ref[0], preferred_element_type=F32) + b_ref[0]


def _mod_vectors(cvec, mod_w, mod_b):
    depth, d, n = mod_w.shape
    bn = 1536
    return pl.pallas_call(
        _mod_kernel,
        grid=(depth, n // bn),
        in_specs=[pl.BlockSpec((8, d), lambda i, j: (0, 0)),
                  pl.BlockSpec((1, d, bn), lambda i, j: (i, 0, j)),
                  pl.BlockSpec((1, 1, bn), lambda i, j: (i, 0, j))],
        out_specs=pl.BlockSpec((1, 8, bn), lambda i, j: (i, 0, j)),
        out_shape=jax.ShapeDtypeStruct((depth, 8, n), F32),
        compiler_params=_cparams(2),
        name="adaln_vectors",
    )(cvec, mod_w, mod_b.reshape(depth, 1, n))


def _ffn_kernel(*refs, mod_base, attn, final, n_cast):
    refs = list(refs)
    wg_ref, wu_ref, wd_ref = refs[-3:]
    o_ref = refs[-4]
    x_ref = refs.pop(0)
    if attn:
        a_ref, d_ref, wo_ref = refs.pop(0), refs.pop(0), refs.pop(0)
    mod_ref, g_ref, wg32_ref, wu32_ref, wd32_ref = refs[:5]
    step = pl.program_id(0)

    for c in range(n_cast):
        @pl.when(step == c)
        def _(c=c):
            sl = slice(c * FF_CAST_CHUNK, (c + 1) * FF_CAST_CHUNK)
            wg_ref[:, sl] = wg32_ref[...].astype(BF16)
            wu_ref[:, sl] = wu32_ref[...].astype(BF16)
            wd_ref[sl, :] = wd32_ref[...].astype(BF16)

    @pl.when(step >= n_cast)
    def _():
        shift = mod_ref[0, mod_base:mod_base + 1, :]
        scale = mod_ref[0, mod_base + 1:mod_base + 2, :]
        gate = mod_ref[0, mod_base + 2:mod_base + 3, :]
        blocks = [slice(r, r + FFN_ROW_BLOCK) for r in range(0, x_ref.shape[0], FFN_ROW_BLOCK)]
        xs, xns = [], []
        for rs in blocks:
            x = x_ref[rs, :]
            if attn:
                y = (jnp.dot(a_ref[rs, :], wo_ref[0:HALF_WIDTH, :], preferred_element_type=F32)
                     + jnp.dot(d_ref[rs, :], wo_ref[HALF_WIDTH:2 * HALF_WIDTH, :], preferred_element_type=F32))
                x = x + mod_ref[0, 5:6, :] * y
            xs.append(x)
            xns.append(_modulate(x, g_ref[...], shift, scale).astype(BF16))
        for rs, x, xn in zip(blocks, xs, xns):
            y = None
            for lo, width in FF_CHUNKS:
                hg = jnp.dot(xn, wg_ref[:, lo:lo + width], preferred_element_type=F32)
                hu = jnp.dot(xn, wu_ref[:, lo:lo + width], preferred_element_type=F32)
                a = (_silu(hg) * hu).astype(BF16)
                part = jnp.dot(a, wd_ref[lo:lo + width, :], preferred_element_type=F32)
                y = part if y is None else y + part
            out = x + (0.5 * gate) * y
            if final:
                fg = refs[5][...]
                ms = jnp.mean(out * out, axis=-1, keepdims=True)
                out = out * lax.rsqrt(ms + EPS) * fg
            o_ref[rs, :] = out


def _ffn(x, mods, g, weights, layer, which, *, tile, mod_base, group_of_tile, attn=None, final_g=None):
    rows, d = x.shape
    n_tiles = rows // tile
    wg, wu, wd = weights
    f = wg.shape[-1]
    n_cast = f // FF_CAST_CHUNK
    row_tile = lambda s: jnp.maximum(s - n_cast, 0)
    slab = lambda s: jnp.minimum(s, n_cast - 1)
    in_specs = [pl.BlockSpec((tile, d), lambda s: (row_tile(s), 0))]
    args = [x]
    if attn is not None:
        a, dd, w_out = attn
        in_specs += [pl.BlockSpec((tile, HALF_WIDTH), lambda s: (row_tile(s), 0)),
                     pl.BlockSpec((tile, HALF_WIDTH), lambda s: (row_tile(s), 0)),
                     _resident((2 * HALF_WIDTH, d), lambda s: (0, 0))]
        args += [a, dd, w_out]
    in_specs += [pl.BlockSpec((1, N_MOD, d), lambda s: (group_of_tile(row_tile(s)), 0, 0)),
                 _resident((1, d), lambda s: (0, 0)),
                 pl.BlockSpec((None, None, d, FF_CAST_CHUNK), lambda s: (layer, which, 0, slab(s))),
                 pl.BlockSpec((None, None, d, FF_CAST_CHUNK), lambda s: (layer, which, 0, slab(s))),
                 pl.BlockSpec((None, None, FF_CAST_CHUNK, d), lambda s: (layer, which, slab(s), 0))]
    args += [mods, g.reshape(1, d), wg, wu, wd]
    if final_g is not None:
        in_specs.append(_resident((1, d), lambda s: (0, 0)))
        args.append(final_g.reshape(1, d))
    return pl.pallas_call(
        functools.partial(_ffn_kernel, mod_base=mod_base, attn=attn is not None, final=final_g is not None,
                          n_cast=n_cast),
        grid=(n_cast + n_tiles,),
        in_specs=in_specs,
        out_specs=pl.BlockSpec((tile, d), lambda s: (row_tile(s), 0)),
        out_shape=jax.ShapeDtypeStruct((rows, d), F32),
        scratch_shapes=[pltpu.VMEM((d, f), BF16), pltpu.VMEM((d, f), BF16), pltpu.VMEM((f, d), BF16)],
        compiler_params=_cparams(1),
        name="macaron_ffn",
    )(*args)


def _rope(z, cos, sin_next, sin_prev):
    return z * cos + pltpu.roll(z, LANES - 1, 1) * sin_next + pltpu.roll(z, 1, 1) * sin_prev


def _rope_tile_tables(row_ref, col_ref, is_ctx):
    grid_rows = ROW_TILE // GRID_W
    lane = lax.broadcasted_iota(jnp.int32, (ROW_TILE, LANES), 1)
    by_row = (lane % HEAD_DIM) // 2 < HEAD_DIM // 4
    tabs = []
    for k, identity in enumerate((1.0, 0.0, 0.0)):
        rows = row_ref[k, 0]
        row_part = jnp.concatenate([jnp.broadcast_to(rows[r:r + 1], (GRID_W, LANES)) for r in range(grid_rows)],
                                   axis=0)
        col_part = jnp.concatenate([col_ref[k]] * grid_rows, axis=0)
        tabs.append(jnp.where(is_ctx, identity, jnp.where(by_row, row_part, col_part)))
    return tabs


def _attn_in_kernel(x_ref, xc_ref, mod_ref, g_ref, w_ref, row_ref, col_ref,
                    qa_ref, ka_ref, va_ref, qbt_ref, kb_ref, vbt_ref, *, tiles_per_batch):
    is_ctx = pl.program_id(0) % tiles_per_batch == tiles_per_batch - 1
    x = jnp.where(is_ctx, xc_ref[...], x_ref[...])
    xn = _modulate(x, g_ref[...], mod_ref[0, 3:4, :], mod_ref[0, 4:5, :]).astype(BF16)
    w = HALF_WIDTH
    scale = HEAD_DIM ** -0.5 * LOG2_E
    qa_ref[...] = (jnp.dot(xn, w_ref[:, 0:w], preferred_element_type=F32) * scale).astype(BF16)
    ka_ref[...] = jnp.dot(xn, w_ref[:, w:2 * w], preferred_element_type=F32).astype(BF16)
    va_ref[...] = jnp.dot(xn, w_ref[:, 2 * w:3 * w], preferred_element_type=F32).astype(BF16)
    cos, sn, sp = _rope_tile_tables(row_ref, col_ref, is_ctx)
    qb = jnp.dot(xn, w_ref[:, 3 * w:4 * w], preferred_element_type=F32)
    kb = jnp.dot(xn, w_ref[:, 4 * w:5 * w], preferred_element_type=F32)
    for h in range(DIFF_HEADS):
        sl = slice(h * LANES, (h + 1) * LANES)
        qbt_ref[0, sl, :] = (_rope(qb[:, sl], cos, sn, sp) * scale).T.astype(BF16)
        kb_ref[:, sl] = _rope(kb[:, sl], cos, sn, sp).astype(BF16)
    vbt = jnp.dot(xn, w_ref[:, 5 * w:6 * w], preferred_element_type=F32).T.astype(BF16)
    pad_rows = DIFF_V_ROWS - DIFF_V_DIM
    ones_row = (lax.broadcasted_iota(jnp.int32, (pad_rows, vbt.shape[1]), 0) == 0).astype(BF16)
    for h in range(DIFF_HEADS):
        vbt_ref[0, h * DIFF_V_ROWS:h * DIFF_V_ROWS + DIFF_V_DIM, :] = vbt[h * DIFF_V_DIM:(h + 1) * DIFF_V_DIM]
        vbt_ref[0, h * DIFF_V_ROWS + DIFF_V_DIM:(h + 1) * DIFF_V_ROWS, :] = ones_row


def _rope_tables(n_lat):
    n_freq = HEAD_DIM // 4
    grid_rows = ROW_TILE // GRID_W
    lane = jnp.arange(LANES)
    pair = (lane % HEAD_DIM) // 2
    inv = ROPE_THETA ** (-(pair % n_freq).astype(F32) / n_freq)
    even = lane % 2 == 0

    def tables(pos):
        ang = pos.astype(F32)[:, None] * inv[None, :]
        sin = jnp.sin(ang)
        return jnp.stack([jnp.cos(ang), jnp.where(even, -sin, 0.0), jnp.where(even, 0.0, sin)])

    n_rows = n_lat // GRID_W
    row_tabs = tables(jnp.arange(n_rows)).reshape(3, n_rows // grid_rows, grid_rows, LANES)
    row_tabs = jnp.pad(row_tabs, ((0, 0), (0, 0), (0, 8 - grid_rows), (0, 0)))
    return row_tabs, tables(jnp.arange(GRID_W))


def _attn_in(x_lat, x_ctx, mods, g, w_in, tables, *, batch, tiles_per_batch):
    d = x_lat.shape[1]
    lat_tiles = tiles_per_batch - 1
    n_tiles = batch * tiles_per_batch
    rows = n_tiles * ROW_TILE

    def group(i):
        return jnp.where(i % tiles_per_batch == lat_tiles, batch, i // tiles_per_batch)

    def lat_first(i):
        b, j = i // tiles_per_batch, i % tiles_per_batch
        return jnp.where(j == lat_tiles, batch * lat_tiles + b, b * lat_tiles + j)

    half = jax.ShapeDtypeStruct((rows, HALF_WIDTH), BF16)
    q_spec = pl.BlockSpec((ROW_TILE, HALF_WIDTH), lambda i: (lat_first(i), 0))
    kv_spec = pl.BlockSpec((ROW_TILE, HALF_WIDTH), lambda i: (i, 0))
    def lat_tile(i):
        return (i // tiles_per_batch) * lat_tiles + jnp.minimum(i % tiles_per_batch, lat_tiles - 1)

    return pl.pallas_call(
        functools.partial(_attn_in_kernel, tiles_per_batch=tiles_per_batch),
        grid=(n_tiles,),
        in_specs=[pl.BlockSpec((ROW_TILE, d), lambda i: (lat_tile(i), 0)),
                  pl.BlockSpec((ROW_TILE, d), lambda i: (i // tiles_per_batch, 0)),
                  pl.BlockSpec((1, N_MOD, d), lambda i: (group(i), 0, 0)),
                  _resident((1, d), lambda i: (0, 0)),
                  _resident((d, ATTN_IN_WIDTH), lambda i: (0, 0)),
                  pl.BlockSpec((3, 1, 8, LANES), lambda i: (0, jnp.minimum(i % tiles_per_batch, lat_tiles - 1), 0, 0)),
                  _resident((3, GRID_W, LANES), lambda i: (0, 0, 0))],
        out_specs=[q_spec, kv_spec, kv_spec,
                   pl.BlockSpec((1, HALF_WIDTH, ROW_TILE), lambda i: (lat_first(i), 0, 0)), kv_spec,
                   pl.BlockSpec((1, DIFF_HEADS * DIFF_V_ROWS, ROW_TILE), lambda i: (i, 0, 0))],
        out_shape=[half, half, half,
                   jax.ShapeDtypeStruct((n_tiles, HALF_WIDTH, ROW_TILE), BF16), half,
                   jax.ShapeDtypeStruct((n_tiles, DIFF_HEADS * DIFF_V_ROWS, ROW_TILE), BF16)],
        compiler_params=_cparams(1),
        name="attn_in_proj",
    )(x_lat, x_ctx, mods, g.reshape(1, d), w_in, *tables)


NA_PAIR_PAD = 4


def _na_bias_pairs(rpb):
    heads = rpb.shape[0]
    qc = np.arange(GRID_W)
    col_start = np.clip(qc - NA_WIN_COLS // 2, 0, GRID_W - NA_WIN_COLS)
    col_ok = (qc[None, :] >= col_start[:, None]) & (qc[None, :] < col_start[:, None] + NA_WIN_COLS)
    rpb = rpb.astype(F32) * LOG2_E
    dc = np.clip(qc[None, :] - qc[:, None] + NA_WIN_COLS - 1, 0, 2 * NA_WIN_COLS - 2)
    onehot = (dc[None] == np.arange(2 * NA_WIN_COLS - 1)[:, None, None]) & col_ok[None]
    t = jnp.einsum('hdc,cqk->hdqk', rpb, jnp.asarray(onehot, F32), precision=lax.Precision.HIGHEST)
    t = jnp.where(col_ok[None, None], t, NEG_INF)
    t = jnp.pad(t, ((0, 0), (NA_PAIR_PAD, NA_PAIR_PAD), (0, 0), (0, 0)), constant_values=NEG_INF)
    return jnp.concatenate([t[:, :-1], t[:, 1:]], axis=-1)


def _na_fill_bias(pair_ref, bias_ref, rows):
    kr_win = min(NA_WIN_ROWS, rows)
    low = lax.broadcasted_iota(jnp.int32, (GRID_W, LANES), 1) < GRID_W
    for v, (r0, ws) in enumerate(((0, 0), (NA_Q_ROWS, 0), (rows - NA_Q_ROWS, rows - NA_BAND_ROWS))):
        for i in range(NA_Q_ROWS):
            qr = r0 + i
            start = min(max(qr - kr_win // 2, 0), rows - kr_win)
            for jp in range(NA_BAND_ROWS // 2):
                kr = ws + 2 * jp
                ok = [start <= kr + e < start + kr_win for e in range(2)]
                entry = kr - qr + NA_WIN_ROWS - 1 + NA_PAIR_PAD
                for h in range(NA_HEADS):
                    if not any(ok):
                        blk = jnp.full((GRID_W, LANES), NEG_INF, F32)
                    else:
                        blk = pair_ref[h, entry]
                        if not all(ok):
                            blk = jnp.where(low if ok[0] else jnp.logical_not(low), blk, NEG_INF)
                    bias_ref[v, h, i * GRID_W:(i + 1) * GRID_W, jp * LANES:(jp + 1) * LANES] = blk


def _na_kernel(q_ref, k_ref, v_ref, pair_ref, o_ref, bias_ref, *, n_lat, n_ctx, rows, groups):
    g = pl.program_id(1)

    @pl.when(jnp.logical_and(pl.program_id(0) == 0, g == 0))
    def _():
        _na_fill_bias(pair_ref, bias_ref, rows)

    variant = jnp.where(g == 0, 0, jnp.where(g == groups - 1, 2, 1))
    ws = jnp.clip(NA_Q_ROWS * g - NA_WIN_ROWS // 2, 0, rows - NA_BAND_ROWS)
    start = pl.multiple_of(ws * GRID_W, GRID_W)
    band = NA_BAND_ROWS * GRID_W
    tq = NA_Q_ROWS * GRID_W
    low = lax.broadcasted_iota(jnp.int32, (tq, LANES), 1) < HEAD_DIM

    def scores(h):
        sl = slice(h // 2 * LANES, (h // 2 + 1) * LANES)
        q = q_ref[:, sl]
        qm = jnp.where(low if h % 2 == 0 else jnp.logical_not(low), q, jnp.zeros_like(q))
        sb = lax.dot_general(qm, k_ref[pl.ds(start, band), sl], NT_DIMS, preferred_element_type=F32)
        sc = lax.dot_general(qm, k_ref[n_lat:n_lat + n_ctx, sl], NT_DIMS, preferred_element_type=F32)
        return sb + bias_ref[variant, h], sc

    def attend(h, sb, sc):
        sl = slice(h // 2 * LANES, (h // 2 + 1) * LANES)
        m = jnp.maximum(jnp.max(sb, axis=-1, keepdims=True), jnp.max(sc, axis=-1, keepdims=True))
        pb = jnp.exp2(sb - m)
        pc = jnp.exp2(sc - m)
        l = jnp.sum(pb, axis=-1, keepdims=True) + jnp.sum(pc, axis=-1, keepdims=True)
        o = (jnp.dot(pb.astype(BF16), v_ref[pl.ds(start, band), sl], preferred_element_type=F32)
             + jnp.dot(pc.astype(BF16), v_ref[n_lat:n_lat + n_ctx, sl], preferred_element_type=F32))
        return o * (1.0 / l)

    s_cur = scores(0)
    even_out = None
    for h in range(NA_HEADS):
        s_nxt = scores(h + 1) if h + 1 < NA_HEADS else None
        o = attend(h, *s_cur)
        if h % 2 == 0:
            even_out = o
        else:
            sl = slice(h // 2 * LANES, (h // 2 + 1) * LANES)
            o_ref[:, sl] = jnp.where(low, even_out, o).astype(BF16)
        s_cur = s_nxt


def _na_attention(qa, ka, va, bias_pairs, *, batch, n_lat, n_ctx):
    rows = n_lat // GRID_W
    groups = rows // NA_Q_ROWS
    tq = NA_Q_ROWS * GRID_W
    per_batch = n_lat + n_ctx
    kv_spec = _resident((per_batch, HALF_WIDTH), lambda b, g: (b, 0))
    return pl.pallas_call(
        functools.partial(_na_kernel, n_lat=n_lat, n_ctx=n_ctx, rows=rows, groups=groups),
        grid=(batch, groups),
        in_specs=[pl.BlockSpec((tq, HALF_WIDTH), lambda b, g: (b * groups + g, 0)),
                  kv_spec, kv_spec,
                  _resident(bias_pairs.shape, lambda b, g: (0, 0, 0, 0))],
        out_specs=pl.BlockSpec((tq, HALF_WIDTH), lambda b, g: (b * groups + g, 0)),
        out_shape=jax.ShapeDtypeStruct((batch * n_lat, HALF_WIDTH), BF16),
        scratch_shapes=[pltpu.VMEM((3, NA_HEADS, tq, NA_BAND_ROWS * GRID_W), F32)],
        compiler_params=_cparams(2),
        name="neighbourhood_attention",
    )(qa, ka, va, bias_pairs)


def _diff_kernel(q_ref, qn_ref, k_ref, vt_ref, lam_ref, g_ref, o_ref, qc_ref, s_ref, m_ref, alpha_ref, acc_ref,
                 *, n_chunks, tiles_per_chunk, lam_init):
    tq = q_ref.shape[0] * q_ref.shape[2]
    tk = tiles_per_chunk * DIFF_K_TILE

    def load_q(ref):
        qt = jnp.concatenate([ref[j] for j in range(ref.shape[0])], axis=1)
        low = lax.broadcasted_iota(jnp.int32, qt.shape, 0) < HEAD_DIM
        zero = jnp.zeros_like(qt)
        qc_ref[:, 0:tq] = jnp.where(low, qt, zero)
        qc_ref[:, tq:2 * tq] = jnp.where(low, zero, qt)

    def tile_scores(c, t):
        row0 = pl.multiple_of(c * tk + t * DIFF_K_TILE, DIFF_K_TILE)
        s = jnp.dot(k_ref[pl.ds(row0, DIFF_K_TILE), :], qc_ref[...], preferred_element_type=F32)
        s_ref[t * DIFF_K_TILE:(t + 1) * DIFF_K_TILE, :] = s
        return jnp.max(s, axis=0, keepdims=True)

    def finish_max(cmax, fresh):
        if fresh:
            alpha_ref[...] = jnp.zeros(alpha_ref.shape, F32)
            m_ref[...] = cmax
        else:
            m_old = m_ref[...]
            m_new = jnp.maximum(m_old, cmax)
            alpha_ref[...] = jnp.exp2(m_old - m_new)
            m_ref[...] = m_new

    def step(c, next_chunk, fresh):
        m = m_ref[...]
        alpha = alpha_ref[...]
        cmax = None
        for t in range(tiles_per_chunk):
            p = jnp.exp2(s_ref[t * DIFF_K_TILE:(t + 1) * DIFF_K_TILE, :] - m).astype(BF16)
            tmax = tile_scores(next_chunk, t)
            cmax = tmax if cmax is None else jnp.maximum(cmax, tmax)
            pv = jnp.dot(vt_ref[c * tiles_per_chunk + t], p, preferred_element_type=F32)
            acc_ref[...] = (alpha * acc_ref[...] if t == 0 else acc_ref[...]) + pv
        finish_max(cmax, fresh)

    load_q(q_ref)
    acc_ref[...] = jnp.zeros(acc_ref.shape, F32)

    @pl.when(pl.program_id(2) == 0)
    def _():
        cmax = None
        for t in range(tiles_per_chunk):
            tmax = tile_scores(0, t)
            cmax = tmax if cmax is None else jnp.maximum(cmax, tmax)
        finish_max(cmax, True)

    def body(c, carry):
        step(c, c + 1, False)
        return carry

    lax.fori_loop(0, n_chunks - 1, body, 0, unroll=DIFF_UNROLL)
    load_q(qn_ref)
    step(n_chunks - 1, 0, True)
    lv = lam_ref[...]
    lam = (jnp.exp(jnp.sum(lv[0:1] * lv[1:2], axis=-1, keepdims=True))
           - jnp.exp(jnp.sum(lv[2:3] * lv[3:4], axis=-1, keepdims=True)) + lam_init)
    o = acc_ref[0:DIFF_V_DIM, :] * (1.0 / acc_ref[DIFF_V_DIM:DIFF_V_DIM + 1, :])
    d = o[:, 0:tq] - lam * o[:, tq:2 * tq]
    ms = jnp.mean(d * d, axis=0, keepdims=True)
    d = d * lax.rsqrt(ms + EPS) * g_ref[...] * (1.0 - lam_init)
    o_ref[...] = d.T.astype(BF16)


def _diff_attention(qbt, kb, vbt, lam_vec, subln_g, *, batch, n_q, q_row0, k_tile0, k_tiles,
                    tiles_per_batch, q_tile, lam_init):
    q_tiles = n_q // q_tile
    q_blk0 = q_row0 // q_tile
    q_sub = q_tile // ROW_TILE
    tiles_per_chunk = DIFF_CHUNK_TILES if k_tiles % DIFF_CHUNK_TILES == 0 else 1
    n_chunks = k_tiles // tiles_per_chunk
    tk = tiles_per_chunk * DIFF_K_TILE
    if k_tile0 == 0:
        k_spec = _resident((k_tiles * DIFF_K_TILE, LANES), lambda b, h, i: (b, h))
    else:
        k_spec = _resident((k_tiles * DIFF_K_TILE, LANES), lambda b, h, i: (b * tiles_per_batch + k_tile0, h))
    return pl.pallas_call(
        functools.partial(_diff_kernel, n_chunks=n_chunks, tiles_per_chunk=tiles_per_chunk, lam_init=lam_init),
        grid=(batch, DIFF_HEADS, q_tiles),
        in_specs=[pl.BlockSpec((q_sub, LANES, ROW_TILE), lambda b, h, i: (q_blk0 + b * q_tiles + i, h, 0)),
                  pl.BlockSpec((q_sub, LANES, ROW_TILE),
                               lambda b, h, i: (q_blk0 + b * q_tiles + jnp.minimum(i + 1, q_tiles - 1), h, 0)),
                  k_spec,
                  _resident((k_tiles, DIFF_V_ROWS, DIFF_K_TILE),
                            lambda b, h, i: ((b * tiles_per_batch + k_tile0) // k_tiles, h, 0)),
                  _resident((4, HEAD_DIM), lambda b, h, i: (0, 0)),
                  _resident((DIFF_V_DIM, 1), lambda b, h, i: (0, 0))],
        out_specs=pl.BlockSpec((q_tile, LANES), lambda b, h, i: (b * q_tiles + i, h)),
        out_shape=jax.ShapeDtypeStruct((batch * n_q, HALF_WIDTH), BF16),
        scratch_shapes=[pltpu.VMEM((LANES, 2 * q_tile), BF16),
                        pltpu.VMEM((tk, 2 * q_tile), F32),
                        pltpu.VMEM((1, 2 * q_tile), F32),
                        pltpu.VMEM((1, 2 * q_tile), F32),
                        pltpu.VMEM((DIFF_V_ROWS, 2 * q_tile), F32)],
        compiler_params=_cparams(3),
        name="differential_attention",
    )(qbt, qbt, kb, vbt, lam_vec, subln_g.reshape(DIFF_V_DIM, 1))


def _dense_kernel(q_ref, k_ref, v_ref, o_ref):
    tq = q_ref.shape[0]
    low = lax.broadcasted_iota(jnp.int32, (tq, LANES), 1) < HEAD_DIM
    for p in range(NA_HEADS // 2):
        sl = slice(p * LANES, (p + 1) * LANES)
        q, k, v = q_ref[:, sl], k_ref[:, sl], v_ref[:, sl]
        outs = []
        for e in range(2):
            qm = jnp.where(low if e == 0 else jnp.logical_not(low), q, jnp.zeros_like(q))
            s = lax.dot_general(qm, k, NT_DIMS, preferred_element_type=F32)
            pr = jnp.exp2(s - jnp.max(s, axis=-1, keepdims=True))
            o = jnp.dot(pr.astype(BF16), v, preferred_element_type=F32)
            outs.append(o * (1.0 / jnp.sum(pr, axis=-1, keepdims=True)))
        o_ref[:, sl] = jnp.where(low, outs[0], outs[1]).astype(BF16)


def _dense_attention(qa, ka, va, *, batch, q_row0, k_tile0, tiles_per_batch):
    q_blk0 = q_row0 // ROW_TILE
    kv_spec = pl.BlockSpec((ROW_TILE, HALF_WIDTH), lambda b: (b * tiles_per_batch + k_tile0, 0))
    return pl.pallas_call(
        _dense_kernel,
        grid=(batch,),
        in_specs=[pl.BlockSpec((ROW_TILE, HALF_WIDTH), lambda b: (q_blk0 + b, 0)), kv_spec, kv_spec],
        out_specs=pl.BlockSpec((ROW_TILE, HALF_WIDTH), lambda b: (b, 0)),
        out_shape=jax.ShapeDtypeStruct((batch * ROW_TILE, HALF_WIDTH), BF16),
        compiler_params=_cparams(1),
        name="context_attention",
    )(qa, ka, va)


def _conv_kernel(x_ref, xp_ref, xs_ref, mod_ref, g_ref, w_in_ref, cw_ref, w_out_ref, o_ref, *, tiles_per_seq):
    i = pl.program_id(0)
    tile, d = x_ref.shape
    halo = xp_ref.shape[0]
    x = x_ref[...]
    x_ext = jnp.concatenate([xp_ref[...], x, xs_ref[...]], axis=0)
    xn = _modulate(x_ext, g_ref[...], mod_ref[0, 3:4, :], mod_ref[0, 4:5, :]).astype(BF16)
    cg = jnp.dot(xn, w_in_ref[:, d:2 * d], preferred_element_type=F32)
    hh = jnp.dot(xn, w_in_ref[:, 2 * d:3 * d], preferred_element_type=F32)
    u = cg * hh
    n_ext = tile + 2 * halo
    ridx = lax.broadcasted_iota(jnp.int32, (n_ext, 1), 0)
    first = (i % tiles_per_seq) == 0
    last = (i % tiles_per_seq) == tiles_per_seq - 1
    u = jnp.where(jnp.logical_or(jnp.logical_and(first, ridx < halo),
                                 jnp.logical_and(last, ridx >= halo + tile)), 0.0, u)
    cw = cw_ref[...]
    u_prev = pltpu.roll(u, 1, 0)[halo:halo + tile]
    u_next = pltpu.roll(u, n_ext - 1, 0)[halo:halo + tile]
    v = u_prev * cw[0:1] + u[halo:halo + tile] * cw[1:2] + u_next * cw[2:3]
    bg = jnp.dot(xn[halo:halo + tile], w_in_ref[:, 0:d], preferred_element_type=F32)
    y = jnp.dot((bg * v).astype(BF16), w_out_ref[...], preferred_element_type=F32)
    o_ref[...] = x + mod_ref[0, 5:6, :] * y


def _conv_mixer(x, mods, group_of_tile, g, w_in, conv_w, w_out, *, tile, seq):
    rows, d = x.shape
    halo = 8
    tiles_per_seq = seq // tile
    hb = tile // halo
    n_hblocks = rows // halo
    return pl.pallas_call(
        functools.partial(_conv_kernel, tiles_per_seq=tiles_per_seq),
        grid=(rows // tile,),
        in_specs=[pl.BlockSpec((tile, d), lambda i: (i, 0)),
                  pl.BlockSpec((halo, d), lambda i: (jnp.maximum(i * hb - 1, 0), 0)),
                  pl.BlockSpec((halo, d), lambda i: (jnp.minimum((i + 1) * hb, n_hblocks - 1), 0)),
                  pl.BlockSpec((1, N_MOD, d), lambda i: (group_of_tile(i), 0, 0)),
                  _resident((1, d), lambda i: (0, 0)),
                  _resident((d, 3 * d), lambda i: (0, 0)),
                  _resident((CONV_WIDTH, d), lambda i: (0, 0)),
                  _resident((d, d), lambda i: (0, 0))],
        out_specs=pl.BlockSpec((tile, d), lambda i: (i, 0)),
        out_shape=jax.ShapeDtypeStruct((rows, d), F32),
        compiler_params=_cparams(1),
        name="conv_mixer",
    )(x, x, x, mods, g.reshape(1, d), w_in, conv_w, w_out)


def kernel(x, c, ctx, c_ctx, mod_w, mod_b, norm_g, ffn_w_gate, ffn_w_up, ffn_w_down, attn_w_in, attn_w_out,
           na_rpb, diff_lambda, diff_subln_g, conv_w_in, conv_w_out, conv_w, final_g):
    batch, n_lat, d = x.shape
    n_ctx = ctx.shape[1]
    depth = mod_w.shape[0]
    assert n_ctx == ROW_TILE and n_lat % FFN_TILE == 0 and depth == 2 and batch + 1 <= 8
    tiles_per_batch = (n_lat + n_ctx) // ROW_TILE
    lat_tiles = n_lat // ROW_TILE
    bf = lambda a: a.astype(BF16)

    cvec = jnp.concatenate([c, c_ctx[None, :], jnp.zeros((8 - batch - 1, d), F32)], axis=0)
    mods_all = _mod_vectors(cvec, mod_w, mod_b).reshape(depth, 8, N_MOD, d)
    ffn_w = (ffn_w_gate, ffn_w_up, ffn_w_down)

    ffn_group = lambda i: i // (n_lat // FFN_TILE)
    conv_group = lambda i: i // (n_lat // LAT_TILE)
    ctx_group = lambda i: batch

    mods = mods_all[0]
    x_lat = _ffn(x.reshape(batch * n_lat, d), mods, norm_g[0, 0], ffn_w, 0, 0, tile=FFN_TILE, mod_base=0,
                 group_of_tile=ffn_group)
    x_ctx = _ffn(ctx.reshape(batch * n_ctx, d), mods, norm_g[0, 0], ffn_w, 0, 0, tile=ROW_TILE, mod_base=0,
                 group_of_tile=ctx_group)
    qa, ka, va, qbt, kb, vbt = _attn_in(x_lat, x_ctx, mods, norm_g[0, 1], bf(attn_w_in[0]),
                                        _rope_tables(n_lat), batch=batch, tiles_per_batch=tiles_per_batch)
    lam_init = 0.8 - 0.6 * math.exp(-0.3 * 0)
    a_lat = _na_attention(qa, ka, va, _na_bias_pairs(na_rpb[0]),
                          batch=batch, n_lat=n_lat, n_ctx=n_ctx)
    d_lat = _diff_attention(qbt, kb, vbt, diff_lambda[0], diff_subln_g[0], batch=batch, n_q=n_lat, q_row0=0,
                            k_tile0=0, k_tiles=tiles_per_batch, tiles_per_batch=tiles_per_batch,
                            q_tile=DIFF_Q_TILE, lam_init=lam_init)
    w_out = bf(attn_w_out[0])
    x_lat = _ffn(x_lat, mods, norm_g[0, 2], ffn_w, 0, 1, tile=FFN_TILE, mod_base=6, group_of_tile=ffn_group,
                 attn=(a_lat, d_lat, w_out))

    a_ctx = _dense_attention(qa, ka, va, batch=batch, q_row0=batch * n_lat, k_tile0=lat_tiles,
                             tiles_per_batch=tiles_per_batch)
    d_ctx = _diff_attention(qbt, kb, vbt, diff_lambda[0], diff_subln_g[0], batch=batch, n_q=n_ctx,
                            q_row0=batch * n_lat, k_tile0=lat_tiles, k_tiles=1,
                            tiles_per_batch=tiles_per_batch, q_tile=ROW_TILE, lam_init=lam_init)
    x_ctx = _ffn(x_ctx, mods, norm_g[0, 2], ffn_w, 0, 1, tile=ROW_TILE, mod_base=6, group_of_tile=ctx_group,
                 attn=(a_ctx, d_ctx, w_out))
    del x_ctx

    mods = mods_all[1]
    x_lat = _ffn(x_lat, mods, norm_g[1, 0], ffn_w, 1, 0, tile=FFN_TILE, mod_base=0, group_of_tile=ffn_group)
    x_lat = _conv_mixer(x_lat, mods, conv_group, norm_g[1, 1], bf(conv_w_in[0]), conv_w[0], bf(conv_w_out[0]),
                        tile=LAT_TILE, seq=n_lat)
    x_lat = _ffn(x_lat, mods, norm_g[1, 2], ffn_w, 1, 1, tile=FFN_TILE, mod_base=6, group_of_tile=ffn_group,
                 final_g=final_g)
    return x_lat.reshape(batch, n_lat, d)
```

```python
import functools
import math

import jax
import jax.numpy as jnp
import numpy as np
from jax import lax
from jax.experimental import pallas as pl
from jax.experimental.pallas import tpu as pltpu

F32 = jnp.float32
BF16 = jnp.bfloat16

GRID_W = 64
HEAD_DIM = 64
NA_HEADS = 8
NA_WIN_ROWS = 8
NA_WIN_COLS = 16
DIFF_HEADS = 4
DIFF_V_DIM = 128
DIFF_V_ROWS = 144
LOG2_E = math.log2(math.e)
HALF_WIDTH = 512
ATTN_IN_WIDTH = 6 * HALF_WIDTH
CONV_WIDTH = 3
ROPE_THETA = 10000.0
N_MOD = 9
EPS = 1e-6
NEG_INF = -1e30

LANES = 128
ROW_TILE = 256
LAT_TILE = 512
FFN_TILE = 512
NA_Q_ROWS = 4
NA_BAND_ROWS = 12
DIFF_Q_TILE = 512
DIFF_K_TILE = 256
DIFF_CHUNK_TILES = 11
DIFF_UNROLL = 2
FF_CHUNKS = ((0, 1024), (1024, 1024), (2048, 768))
FFN_ROW_BLOCK = 256
FF_CAST_CHUNK = 256
VMEM_LIMIT = 56 * 1024 * 1024

NT_DIMS = (((1,), (1,)), ((), ()))


def _cparams(n_axes):
    return pltpu.CompilerParams(dimension_semantics=("arbitrary",) * n_axes,
                                vmem_limit_bytes=VMEM_LIMIT)


def _resident(shape, index_map):
    return pl.BlockSpec(shape, index_map, pipeline_mode=pl.Buffered(1))


def _modulate(x, g, shift, scale):
    ms = jnp.mean(x * x, axis=-1, keepdims=True)
    return x * lax.rsqrt(ms + EPS) * g * (1.0 + scale) + shift


def _silu(x):
    return x * (1.0 / (1.0 + jnp.exp(-x)))


def _mod_kernel(c_ref, w_ref, b_ref, o_ref):
    o_ref[0] = jnp.dot(_silu(c_ref[...]), w_ref[0], preferred_element_type=F32) + b_ref[0]


def _mod_vectors(cvec, mod_w, mod_b):
    depth, d, n = mod_w.shape
    bn = 1536
    return pl.pallas_call(
        _mod_kernel,
        grid=(depth, n // bn),
        in_specs=[pl.BlockSpec((8, d), lambda i, j: (0, 0)),
                  pl.BlockSpec((1, d, bn), lambda i, j: (i, 0, j)),
                  pl.BlockSpec((1, 1, bn), lambda i, j: (i, 0, j))],
        out_specs=pl.BlockSpec((1, 8, bn), lambda i, j: (i, 0, j)),
        out_shape=jax.ShapeDtypeStruct((depth, 8, n), F32),
        compiler_params=_cparams(2),
        name="adaln_vectors",
    )(cvec, mod_w, mod_b.reshape(depth, 1, n))


def _ffn_kernel(*refs, mod_base, attn, final, n_cast):
    refs = list(refs)
    wg_ref, wu_ref, wd_ref = refs[-3:]
    o_ref = refs[-4]
    x_ref = refs.pop(0)
    if attn:
        a_ref, d_ref, wo_ref = refs.pop(0), refs.pop(0), refs.pop(0)
    mod_ref, g_ref, wg32_ref, wu32_ref, wd32_ref = refs[:5]
    step = pl.program_id(0)

    for c in range(n_cast):
        @pl.when(step == c)
        def _(c=c):
            sl = slice(c * FF_CAST_CHUNK, (c + 1) * FF_CAST_CHUNK)
            wg_ref[:, sl] = wg32_ref[...].astype(BF16)
            wu_ref[:, sl] = wu32_ref[...].astype(BF16)
            wd_ref[sl, :] = wd32_ref[...].astype(BF16)

    @pl.when(step >= n_cast)
    def _():
        shift = mod_ref[0, mod_base:mod_base + 1, :]
        scale = mod_ref[0, mod_base + 1:mod_base + 2, :]
        gate = mod_ref[0, mod_base + 2:mod_base + 3, :]
        blocks = [slice(r, r + FFN_ROW_BLOCK) for r in range(0, x_ref.shape[0], FFN_ROW_BLOCK)]
        xs, xns = [], []
        for rs in blocks:
            x = x_ref[rs, :]
            if attn:
                y = (jnp.dot(a_ref[rs, :], wo_ref[0:HALF_WIDTH, :], preferred_element_type=F32)
                     + jnp.dot(d_ref[rs, :], wo_ref[HALF_WIDTH:2 * HALF_WIDTH, :], preferred_element_type=F32))
                x = x + mod_ref[0, 5:6, :] * y
            xs.append(x)
            xns.append(_modulate(x, g_ref[...], shift, scale).astype(BF16))
        for rs, x, xn in zip(blocks, xs, xns):
            y = None
            for lo, width in FF_CHUNKS:
                hg = jnp.dot(xn, wg_ref[:, lo:lo + width], preferred_element_type=F32)
                hu = jnp.dot(xn, wu_ref[:, lo:lo + width], preferred_element_type=F32)
                a = (_silu(hg) * hu).astype(BF16)
                part = jnp.dot(a, wd_ref[lo:lo + width, :], preferred_element_type=F32)
                y = part if y is None else y + part
            out = x + (0.5 * gate) * y
            if final:
                fg = refs[5][...]
                ms = jnp.mean(out * out, axis=-1, keepdims=True)
                out = out * lax.rsqrt(ms + EPS) * fg
            o_ref[rs, :] = out


def _ffn(x, mods, g, weights, layer, which, *, tile, mod_base, group_of_tile, attn=None, final_g=None):
    rows, d = x.shape
    n_tiles = rows // tile
    wg, wu, wd = weights
    f = wg.shape[-1]
    n_cast = f // FF_CAST_CHUNK
    row_tile = lambda s: jnp.maximum(s - n_cast, 0)
    slab = lambda s: jnp.minimum(s, n_cast - 1)
    in_specs = [pl.BlockSpec((tile, d), lambda s: (row_tile(s), 0))]
    args = [x]
    if attn is not None:
        a, dd, w_out = attn
        in_specs += [pl.BlockSpec((tile, HALF_WIDTH), lambda s: (row_tile(s), 0)),
                     pl.BlockSpec((tile, HALF_WIDTH), lambda s: (row_tile(s), 0)),
                     _resident((2 * HALF_WIDTH, d), lambda s: (0, 0))]
        args += [a, dd, w_out]
    in_specs += [pl.BlockSpec((1, N_MOD, d), lambda s: (group_of_tile(row_tile(s)), 0, 0)),
                 _resident((1, d), lambda s: (0, 0)),
                 pl.BlockSpec((None, None, d, FF_CAST_CHUNK), lambda s: (layer, which, 0, slab(s))),
                 pl.BlockSpec((None, None, d, FF_CAST_CHUNK), lambda s: (layer, which, 0, slab(s))),
                 pl.BlockSpec((None, None, FF_CAST_CHUNK, d), lambda s: (layer, which, slab(s), 0))]
    args += [mods, g.reshape(1, d), wg, wu, wd]
    if final_g is not None:
        in_specs.append(_resident((1, d), lambda s: (0, 0)))
        args.append(final_g.reshape(1, d))
    return pl.pallas_call(
        functools.partial(_ffn_kernel, mod_base=mod_base, attn=attn is not None, final=final_g is not None,
                          n_cast=n_cast),
        grid=(n_cast + n_tiles,),
        in_specs=in_specs,
        out_specs=pl.BlockSpec((tile, d), lambda s: (row_tile(s), 0)),
        out_shape=jax.ShapeDtypeStruct((rows, d), F32),
        scratch_shapes=[pltpu.VMEM((d, f), BF16), pltpu.VMEM((d, f), BF16), pltpu.VMEM((f, d), BF16)],
        compiler_params=_cparams(1),
        name="macaron_ffn",
    )(*args)


def _rope(z, cos, sin_next, sin_prev):
    return z * cos + pltpu.roll(z, LANES - 1, 1) * sin_next + pltpu.roll(z, 1, 1) * sin_prev


def _rope_tile_tables(row_ref, col_ref, is_ctx):
    grid_rows = ROW_TILE // GRID_W
    lane = lax.broadcasted_iota(jnp.int32, (ROW_TILE, LANES), 1)
    by_row = (lane % HEAD_DIM) // 2 < HEAD_DIM // 4
    tabs = []
    for k, identity in enumerate((1.0, 0.0, 0.0)):
        rows = row_ref[k, 0]
        row_part = jnp.concatenate([jnp.broadcast_to(rows[r:r + 1], (GRID_W, LANES)) for r in range(grid_rows)],
                                   axis=0)
        col_part = jnp.concatenate([col_ref[k]] * grid_rows, axis=0)
        tabs.append(jnp.where(is_ctx, identity, jnp.where(by_row, row_part, col_part)))
    return tabs


def _attn_in_kernel(x_ref, xc_ref, mod_ref, g_ref, w_ref, row_ref, col_ref,
                    qa_ref, ka_ref, va_ref, qbt_ref, kb_ref, vbt_ref, *, tiles_per_batch):
    is_ctx = pl.program_id(0) % tiles_per_batch == tiles_per_batch - 1
    x = jnp.where(is_ctx, xc_ref[...], x_ref[...])
    xn = _modulate(x, g_ref[...], mod_ref[0, 3:4, :], mod_ref[0, 4:5, :]).astype(BF16)
    w = HALF_WIDTH
    scale = HEAD_DIM ** -0.5 * LOG2_E
    qa_ref[...] = (jnp.dot(xn, w_ref[:, 0:w], preferred_element_type=F32) * scale).astype(BF16)
    ka_ref[...] = jnp.dot(xn, w_ref[:, w:2 * w], preferred_element_type=F32).astype(BF16)
    va_ref[...] = jnp.dot(xn, w_ref[:, 2 * w:3 * w], preferred_element_type=F32).astype(BF16)
    cos, sn, sp = _rope_tile_tables(row_ref, col_ref, is_ctx)
    qb = jnp.dot(xn, w_ref[:, 3 * w:4 * w], preferred_element_type=F32)
    kb = jnp.dot(xn, w_ref[:, 4 * w:5 * w], preferred_element_type=F32)
    for h in range(DIFF_HEADS):
        sl = slice(h * LANES, (h + 1) * LANES)
        qbt_ref[0, sl, :] = (_rope(qb[:, sl], cos, sn, sp) * scale).T.astype(BF16)
        kb_ref[:, sl] = _rope(kb[:, sl], cos, sn, sp).astype(BF16)
    vbt = jnp.dot(xn, w_ref[:, 5 * w:6 * w], preferred_element_type=F32).T.astype(BF16)
    pad_rows = DIFF_V_ROWS - DIFF_V_DIM
    ones_row = (lax.broadcasted_iota(jnp.int32, (pad_rows, vbt.shape[1]), 0) == 0).astype(BF16)
    for h in range(DIFF_HEADS):
        vbt_ref[0, h * DIFF_V_ROWS:h * DIFF_V_ROWS + DIFF_V_DIM, :] = vbt[h * DIFF_V_DIM:(h + 1) * DIFF_V_DIM]
        vbt_ref[0, h * DIFF_V_ROWS + DIFF_V_DIM:(h + 1) * DIFF_V_ROWS, :] = ones_row


def _rope_tables(n_lat):
    n_freq = HEAD_DIM // 4
    grid_rows = ROW_TILE // GRID_W
    lane = jnp.arange(LANES)
    pair = (lane % HEAD_DIM) // 2
    inv = ROPE_THETA ** (-(pair % n_freq).astype(F32) / n_freq)
    even = lane % 2 == 0

    def tables(pos):
        ang = pos.astype(F32)[:, None] * inv[None, :]
        sin = jnp.sin(ang)
        return jnp.stack([jnp.cos(ang), jnp.where(even, -sin, 0.0), jnp.where(even, 0.0, sin)])

    n_rows = n_lat // GRID_W
    row_tabs = tables(jnp.arange(n_rows)).reshape(3, n_rows // grid_rows, grid_rows, LANES)
    row_tabs = jnp.pad(row_tabs, ((0, 0), (0, 0), (0, 8 - grid_rows), (0, 0)))
    return row_tabs, tables(jnp.arange(GRID_W))


def _attn_in(x_lat, x_ctx, mods, g, w_in, tables, *, batch, tiles_per_batch):
    d = x_lat.shape[1]
    lat_tiles = tiles_per_batch - 1
    n_tiles = batch * tiles_per_batch
    rows = n_tiles * ROW_TILE

    def group(i):
        return jnp.where(i % tiles_per_batch == lat_tiles, batch, i // tiles_per_batch)

    def lat_first(i):
        b, j = i // tiles_per_batch, i % tiles_per_batch
        return jnp.where(j == lat_tiles, batch * lat_tiles + b, b * lat_tiles + j)

    half = jax.ShapeDtypeStruct((rows, HALF_WIDTH), BF16)
    q_spec = pl.BlockSpec((ROW_TILE, HALF_WIDTH), lambda i: (lat_first(i), 0))
    kv_spec = pl.BlockSpec((ROW_TILE, HALF_WIDTH), lambda i: (i, 0))
    def lat_tile(i):
        return (i // tiles_per_batch) * lat_tiles + jnp.minimum(i % tiles_per_batch, lat_tiles - 1)

    return pl.pallas_call(
        functools.partial(_attn_in_kernel, tiles_per_batch=tiles_per_batch),
        grid=(n_tiles,),
        in_specs=[pl.BlockSpec((ROW_TILE, d), lambda i: (lat_tile(i), 0)),
                  pl.BlockSpec((ROW_TILE, d), lambda i: (i // tiles_per_batch, 0)),
                  pl.BlockSpec((1, N_MOD, d), lambda i: (group(i), 0, 0)),
                  _resident((1, d), lambda i: (0, 0)),
                  _resident((d, ATTN_IN_WIDTH), lambda i: (0, 0)),
                  pl.BlockSpec((3, 1, 8, LANES), lambda i: (0, jnp.minimum(i % tiles_per_batch, lat_tiles - 1), 0, 0)),
                  _resident((3, GRID_W, LANES), lambda i: (0, 0, 0))],
        out_specs=[q_spec, kv_spec, kv_spec,
                   pl.BlockSpec((1, HALF_WIDTH, ROW_TILE), lambda i: (lat_first(i), 0, 0)), kv_spec,
                   pl.BlockSpec((1, DIFF_HEADS * DIFF_V_ROWS, ROW_TILE), lambda i: (i, 0, 0))],
        out_shape=[half, half, half,
                   jax.ShapeDtypeStruct((n_tiles, HALF_WIDTH, ROW_TILE), BF16), half,
                   jax.ShapeDtypeStruct((n_tiles, DIFF_HEADS * DIFF_V_ROWS, ROW_TILE), BF16)],
        compiler_params=_cparams(1),
        name="attn_in_proj",
    )(x_lat, x_ctx, mods, g.reshape(1, d), w_in, *tables)


NA_PAIR_PAD = 4


def _na_bias_pairs(rpb):
    heads = rpb.shape[0]
    qc = np.arange(GRID_W)
    col_start = np.clip(qc - NA_WIN_COLS // 2, 0, GRID_W - NA_WIN_COLS)
    col_ok = (qc[None, :] >= col_start[:, None]) & (qc[None, :] < col_start[:, None] + NA_WIN_COLS)
    rpb = rpb.astype(F32) * LOG2_E
    dc = np.clip(qc[None, :] - qc[:, None] + NA_WIN_COLS - 1, 0, 2 * NA_WIN_COLS - 2)
    onehot = (dc[None] == np.arange(2 * NA_WIN_COLS - 1)[:, None, None]) & col_ok[None]
    t = jnp.einsum('hdc,cqk->hdqk', rpb, jnp.asarray(onehot, F32), precision=lax.Precision.HIGHEST)
    t = jnp.where(col_ok[None, None], t, NEG_INF)
    t = jnp.pad(t, ((0, 0), (NA_PAIR_PAD, NA_PAIR_PAD), (0, 0), (0, 0)), constant_values=NEG_INF)
    return jnp.concatenate([t[:, :-1], t[:, 1:]], axis=-1)


def _na_fill_bias(pair_ref, bias_ref, rows):
    kr_win = min(NA_WIN_ROWS, rows)
    low = lax.broadcasted_iota(jnp.int32, (GRID_W, LANES), 1) < GRID_W
    for v, (r0, ws) in enumerate(((0, 0), (NA_Q_ROWS, 0), (rows - NA_Q_ROWS, rows - NA_BAND_ROWS))):
        for i in range(NA_Q_ROWS):
            qr = r0 + i
            start = min(max(qr - kr_win // 2, 0), rows - kr_win)
            for jp in range(NA_BAND_ROWS // 2):
                kr = ws + 2 * jp
                ok = [start <= kr + e < start + kr_win for e in range(2)]
                entry = kr - qr + NA_WIN_ROWS - 1 + NA_PAIR_PAD
                for h in range(NA_HEADS):
                    if not any(ok):
                        blk = jnp.full((GRID_W, LANES), NEG_INF, F32)
                    else:
                        blk = pair_ref[h, entry]
                        if not all(ok):
                            blk = jnp.where(low if ok[0] else jnp.logical_not(low), blk, NEG_INF)
                    bias_ref[v, h, i * GRID_W:(i + 1) * GRID_W, jp * LANES:(jp + 1) * LANES] = blk


def _na_kernel(q_ref, k_ref, v_ref, pair_ref, o_ref, bias_ref, *, n_lat, n_ctx, rows, groups):
    g = pl.program_id(1)

    @pl.when(jnp.logical_and(pl.program_id(0) == 0, g == 0))
    def _():
        _na_fill_bias(pair_ref, bias_ref, rows)

    variant = jnp.where(g == 0, 0, jnp.where(g == groups - 1, 2, 1))
    ws = jnp.clip(NA_Q_ROWS * g - NA_WIN_ROWS // 2, 0, rows - NA_BAND_ROWS)
    start = pl.multiple_of(ws * GRID_W, GRID_W)
    band = NA_BAND_ROWS * GRID_W
    tq = NA_Q_ROWS * GRID_W
    low = lax.broadcasted_iota(jnp.int32, (tq, LANES), 1) < HEAD_DIM

    def scores(h):
        sl = slice(h // 2 * LANES, (h // 2 + 1) * LANES)
        q = q_ref[:, sl]
        qm = jnp.where(low if h % 2 == 0 else jnp.logical_not(low), q, jnp.zeros_like(q))
        sb = lax.dot_general(qm, k_ref[pl.ds(start, band), sl], NT_DIMS, preferred_element_type=F32)
        sc = lax.dot_general(qm, k_ref[n_lat:n_lat + n_ctx, sl], NT_DIMS, preferred_element_type=F32)
        return sb + bias_ref[variant, h], sc

    def attend(h, sb, sc):
        sl = slice(h // 2 * LANES, (h // 2 + 1) * LANES)
        m = jnp.maximum(jnp.max(sb, axis=-1, keepdims=True), jnp.max(sc, axis=-1, keepdims=True))
        pb = jnp.exp2(sb - m)
        pc = jnp.exp2(sc - m)
        l = jnp.sum(pb, axis=-1, keepdims=True) + jnp.sum(pc, axis=-1, keepdims=True)
        o = (jnp.dot(pb.astype(BF16), v_ref[pl.ds(start, band), sl], preferred_element_type=F32)
             + jnp.dot(pc.astype(BF16), v_ref[n_lat:n_lat + n_ctx, sl], preferred_element_type=F32))
        return o * (1.0 / l)

    s_cur = scores(0)
    even_out = None
    for h in range(NA_HEADS):
        s_nxt = scores(h + 1) if h + 1 < NA_HEADS else None
        o = attend(h, *s_cur)
        if h % 2 == 0:
            even_out = o
        else:
            sl = slice(h // 2 * LANES, (h // 2 + 1) * LANES)
            o_ref[:, sl] = jnp.where(low, even_out, o).astype(BF16)
        s_cur = s_nxt


def _na_attention(qa, ka, va, bias_pairs, *, batch, n_lat, n_ctx):
    rows = n_lat // GRID_W
    groups = rows // NA_Q_ROWS
    tq = NA_Q_ROWS * GRID_W
    per_batch = n_lat + n_ctx
    kv_spec = _resident((per_batch, HALF_WIDTH), lambda b, g: (b, 0))
    return pl.pallas_call(
        functools.partial(_na_kernel, n_lat=n_lat, n_ctx=n_ctx, rows=rows, groups=groups),
        grid=(batch, groups),
        in_specs=[pl.BlockSpec((tq, HALF_WIDTH), lambda b, g: (b * groups + g, 0)),
                  kv_spec, kv_spec,
                  _resident(bias_pairs.shape, lambda b, g: (0, 0, 0, 0))],
        out_specs=pl.BlockSpec((tq, HALF_WIDTH), lambda b, g: (b * groups + g, 0)),
        out_shape=jax.ShapeDtypeStruct((batch * n_lat, HALF_WIDTH), BF16),
        scratch_shapes=[pltpu.VMEM((3, NA_HEADS, tq, NA_BAND_ROWS * GRID_W), F32)],
        compiler_params=_cparams(2),
        name="neighbourhood_attention",
    )(qa, ka, va, bias_pairs)


def _diff_kernel(q_ref, qn_ref, k_ref, vt_ref, lam_ref, g_ref, o_ref, qc_ref, s_ref, m_ref, alpha_ref, acc_ref,
                 *, n_chunks, tiles_per_chunk, lam_init):
    tq = q_ref.shape[0] * q_ref.shape[2]
    tk = tiles_per_chunk * DIFF_K_TILE

    def load_q(ref):
        qt = jnp.concatenate([ref[j] for j in range(ref.shape[0])], axis=1)
        low = lax.broadcasted_iota(jnp.int32, qt.shape, 0) < HEAD_DIM
        zero = jnp.zeros_like(qt)
        qc_ref[:, 0:tq] = jnp.where(low, qt, zero)
        qc_ref[:, tq:2 * tq] = jnp.where(low, zero, qt)

    def tile_scores(c, t):
        row0 = pl.multiple_of(c * tk + t * DIFF_K_TILE, DIFF_K_TILE)
        s = jnp.dot(k_ref[pl.ds(row0, DIFF_K_TILE), :], qc_ref[...], preferred_element_type=F32)
        s_ref[t * DIFF_K_TILE:(t + 1) * DIFF_K_TILE, :] = s
        return jnp.max(s, axis=0, keepdims=True)

    def finish_max(cmax, fresh):
        if fresh:
            alpha_ref[...] = jnp.zeros(alpha_ref.shape, F32)
            m_ref[...] = cmax
        else:
            m_old = m_ref[...]
            m_new = jnp.maximum(m_old, cmax)
            alpha_ref[...] = jnp.exp2(m_old - m_new)
            m_ref[...] = m_new

    def step(c, next_chunk, fresh):
        m = m_ref[...]
        alpha = alpha_ref[...]
        cmax = None
        for t in range(tiles_per_chunk):
            p = jnp.exp2(s_ref[t * DIFF_K_TILE:(t + 1) * DIFF_K_TILE, :] - m).astype(BF16)
            tmax = tile_scores(next_chunk, t)
            cmax = tmax if cmax is None else jnp.maximum(cmax, tmax)
            pv = jnp.dot(vt_ref[c * tiles_per_chunk + t], p, preferred_element_type=F32)
            acc_ref[...] = (alpha * acc_ref[...] if t == 0 else acc_ref[...]) + pv
        finish_max(cmax, fresh)

    load_q(q_ref)
    acc_ref[...] = jnp.zeros(acc_ref.shape, F32)

    @pl.when(pl.program_id(2) == 0)
    def _():
        cmax = None
        for t in range(tiles_per_chunk):
            tmax = tile_scores(0, t)
            cmax = tmax if cmax is None else jnp.maximum(cmax, tmax)
        finish_max(cmax, True)

    def body(c, carry):
        step(c, c + 1, False)
        return carry

    lax.fori_loop(0, n_chunks - 1, body, 0, unroll=DIFF_UNROLL)
    load_q(qn_ref)
    step(n_chunks - 1, 0, True)
    lv = lam_ref[...]
    lam = (jnp.exp(jnp.sum(lv[0:1] * lv[1:2], axis=-1, keepdims=True))
           - jnp.exp(jnp.sum(lv[2:3] * lv[3:4], axis=-1, keepdims=True)) + lam_init)
    o = acc_ref[0:DIFF_V_DIM, :] * (1.0 / acc_ref[DIFF_V_DIM:DIFF_V_DIM + 1, :])
    d = o[:, 0:tq] - lam * o[:, tq:2 * tq]
    ms = jnp.mean(d * d, axis=0, keepdims=True)
    d = d * lax.rsqrt(ms + EPS) * g_ref[...] * (1.0 - lam_init)
    o_ref[...] = d.T.astype(BF16)


def _diff_attention(qbt, kb, vbt, lam_vec, subln_g, *, batch, n_q, q_row0, k_tile0, k_tiles,
                    tiles_per_batch, q_tile, lam_init):
    q_tiles = n_q // q_tile
    q_blk0 = q_row0 // q_tile
    q_sub = q_tile // ROW_TILE
    tiles_per_chunk = DIFF_CHUNK_TILES if k_tiles % DIFF_CHUNK_TILES == 0 else 1
    n_chunks = k_tiles // tiles_per_chunk
    tk = tiles_per_chunk * DIFF_K_TILE
    if k_tile0 == 0:
        k_spec = _resident((k_tiles * DIFF_K_TILE, LANES), lambda b, h, i: (b, h))
    else:
        k_spec = _resident((k_tiles * DIFF_K_TILE, LANES), lambda b, h, i: (b * tiles_per_batch + k_tile0, h))
    return pl.pallas_call(
        functools.partial(_diff_kernel, n_chunks=n_chunks, tiles_per_chunk=tiles_per_chunk, lam_init=lam_init),
        grid=(batch, DIFF_HEADS, q_tiles),
        in_specs=[pl.BlockSpec((q_sub, LANES, ROW_TILE), lambda b, h, i: (q_blk0 + b * q_tiles + i, h, 0)),
                  pl.BlockSpec((q_sub, LANES, ROW_TILE),
                               lambda b, h, i: (q_blk0 + b * q_tiles + jnp.minimum(i + 1, q_tiles - 1), h, 0)),
                  k_spec,
                  _resident((k_tiles, DIFF_V_ROWS, DIFF_K_TILE),
                            lambda b, h, i: ((b * tiles_per_batch + k_tile0) // k_tiles, h, 0)),
                  _resident((4, HEAD_DIM), lambda b, h, i: (0, 0)),
                  _resident((DIFF_V_DIM, 1), lambda b, h, i: (0, 0))],
        out_specs=pl.BlockSpec((q_tile, LANES), lambda b, h, i: (b * q_tiles + i, h)),
        out_shape=jax.ShapeDtypeStruct((batch * n_q, HALF_WIDTH), BF16),
        scratch_shapes=[pltpu.VMEM((LANES, 2 * q_tile), BF16),
                        pltpu.VMEM((tk, 2 * q_tile), F32),
                        pltpu.VMEM((1, 2 * q_tile), F32),
                        pltpu.VMEM((1, 2 * q_tile), F32),
                        pltpu.VMEM((DIFF_V_ROWS, 2 * q_tile), F32)],
        compiler_params=_cparams(3),
        name="differential_attention",
    )(qbt, qbt, kb, vbt, lam_vec, subln_g.reshape(DIFF_V_DIM, 1))


def _dense_kernel(q_ref, k_ref, v_ref, o_ref):
    tq = q_ref.shape[0]
    low = lax.broadcasted_iota(jnp.int32, (tq, LANES), 1) < HEAD_DIM
    for p in range(NA_HEADS // 2):
        sl = slice(p * LANES, (p + 1) * LANES)
        q, k, v = q_ref[:, sl], k_ref[:, sl], v_ref[:, sl]
        outs = []
        for e in range(2):
            qm = jnp.where(low if e == 0 else jnp.logical_not(low), q, jnp.zeros_like(q))
            s = lax.dot_general(qm, k, NT_DIMS, preferred_element_type=F32)
            pr = jnp.exp2(s - jnp.max(s, axis=-1, keepdims=True))
            o = jnp.dot(pr.astype(BF16), v, preferred_element_type=F32)
            outs.append(o * (1.0 / jnp.sum(pr, axis=-1, keepdims=True)))
        o_ref[:, sl] = jnp.where(low, outs[0], outs[1]).astype(BF16)


def _dense_attention(qa, ka, va, *, batch, q_row0, k_tile0, tiles_per_batch):
    q_blk0 = q_row0 // ROW_TILE
    kv_spec = pl.BlockSpec((ROW_TILE, HALF_WIDTH), lambda b: (b * tiles_per_batch + k_tile0, 0))
    return pl.pallas_call(
        _dense_kernel,
        grid=(batch,),
        in_specs=[pl.BlockSpec((ROW_TILE, HALF_WIDTH), lambda b: (q_blk0 + b, 0)), kv_spec, kv_spec],
        out_specs=pl.BlockSpec((ROW_TILE, HALF_WIDTH), lambda b: (b, 0)),
        out_shape=jax.ShapeDtypeStruct((batch * ROW_TILE, HALF_WIDTH), BF16),
        compiler_params=_cparams(1),
        name="context_attention",
    )(qa, ka, va)


def _conv_kernel(x_ref, xp_ref, xs_ref, mod_ref, g_ref, w_in_ref, cw_ref, w_out_ref, o_ref, *, tiles_per_seq):
    i = pl.program_id(0)
    tile, d = x_ref.shape
    halo = xp_ref.shape[0]
    x = x_ref[...]
    x_ext = jnp.concatenate([xp_ref[...], x, xs_ref[...]], axis=0)
    xn = _modulate(x_ext, g_ref[...], mod_ref[0, 3:4, :], mod_ref[0, 4:5, :]).astype(BF16)
    cg = jnp.dot(xn, w_in_ref[:, d:2 * d], preferred_element_type=F32)
    hh = jnp.dot(xn, w_in_ref[:, 2 * d:3 * d], preferred_element_type=F32)
    u = cg * hh
    n_ext = tile + 2 * halo
    ridx = lax.broadcasted_iota(jnp.int32, (n_ext, 1), 0)
    first = (i % tiles_per_seq) == 0
    last = (i % tiles_per_seq) == tiles_per_seq - 1
    u = jnp.where(jnp.logical_or(jnp.logical_and(first, ridx < halo),
                                 jnp.logical_and(last, ridx >= halo + tile)), 0.0, u)
    cw = cw_ref[...]
    u_prev = pltpu.roll(u, 1, 0)[halo:halo + tile]
    u_next = pltpu.roll(u, n_ext - 1, 0)[halo:halo + tile]
    v = u_prev * cw[0:1] + u[halo:halo + tile] * cw[1:2] + u_next * cw[2:3]
    bg = jnp.dot(xn[halo:halo + tile], w_in_ref[:, 0:d], preferred_element_type=F32)
    y = jnp.dot((bg * v).astype(BF16), w_out_ref[...], preferred_element_type=F32)
    o_ref[...] = x + mod_ref[0, 5:6, :] * y


def _conv_mixer(x, mods, group_of_tile, g, w_in, conv_w, w_out, *, tile, seq):
    rows, d = x.shape
    halo = 8
    tiles_per_seq = seq // tile
    hb = tile // halo
    n_hblocks = rows // halo
    return pl.pallas_call(
        functools.partial(_conv_kernel, tiles_per_seq=tiles_per_seq),
        grid=(rows // tile,),
        in_specs=[pl.BlockSpec((tile, d), lambda i: (i, 0)),
                  pl.BlockSpec((halo, d), lambda i: (jnp.maximum(i * hb - 1, 0), 0)),
                  pl.BlockSpec((halo, d), lambda i: (jnp.minimum((i + 1) * hb, n_hblocks - 1), 0)),
                  pl.BlockSpec((1, N_MOD, d), lambda i: (group_of_tile(i), 0, 0)),
                  _resident((1, d), lambda i: (0, 0)),
                  _resident((d, 3 * d), lambda i: (0, 0)),
                  _resident((CONV_WIDTH, d), lambda i: (0, 0)),
                  _resident((d, d), lambda i: (0, 0))],
        out_specs=pl.BlockSpec((tile, d), lambda i: (i, 0)),
        out_shape=jax.ShapeDtypeStruct((rows, d), F32),
        compiler_params=_cparams(1),
        name="conv_mixer",
    )(x, x, x, mods, g.reshape(1, d), w_in, conv_w, w_out)


def kernel(x, c, ctx, c_ctx, mod_w, mod_b, norm_g, ffn_w_gate, ffn_w_up, ffn_w_down, attn_w_in, attn_w_out,
           na_rpb, diff_lambda, diff_subln_g, conv_w_in, conv_w_out, conv_w, final_g):
    batch, n_lat, d = x.shape
    n_ctx = ctx.shape[1]
    depth = mod_w.shape[0]
    assert n_ctx == ROW_TILE and n_lat % FFN_TILE == 0 and depth == 2 and batch + 1 <= 8
    tiles_per_batch = (n_lat + n_ctx) // ROW_TILE
    lat_tiles = n_lat // ROW_TILE
    bf = lambda a: a.astype(BF16)

    cvec = jnp.concatenate([c, c_ctx[None, :], jnp.zeros((8 - batch - 1, d), F32)], axis=0)
    mods_all = _mod_vectors(cvec, mod_w, mod_b).reshape(depth, 8, N_MOD, d)
    ffn_w = (ffn_w_gate, ffn_w_up, ffn_w_down)

    ffn_group = lambda i: i // (n_lat // FFN_TILE)
    conv_group = lambda i: i // (n_lat // LAT_TILE)
    ctx_group = lambda i: batch

    mods = mods_all[0]
    x_lat = _ffn(x.reshape(batch * n_lat, d), mods, norm_g[0, 0], ffn_w, 0, 0, tile=FFN_TILE, mod_base=0,
                 group_of_tile=ffn_group)
    x_ctx = _ffn(ctx.reshape(batch * n_ctx, d), mods, norm_g[0, 0], ffn_w, 0, 0, tile=ROW_TILE, mod_base=0,
                 group_of_tile=ctx_group)
    qa, ka, va, qbt, kb, vbt = _attn_in(x_lat, x_ctx, mods, norm_g[0, 1], bf(attn_w_in[0]),
                                        _rope_tables(n_lat), batch=batch, tiles_per_batch=tiles_per_batch)
    lam_init = 0.8 - 0.6 * math.exp(-0.3 * 0)
    a_lat = _na_attention(qa, ka, va, _na_bias_pairs(na_rpb[0]),
                          batch=batch, n_lat=n_lat, n_ctx=n_ctx)
    d_lat = _diff_attention(qbt, kb, vbt, diff_lambda[0], diff_subln_g[0], batch=batch, n_q=n_lat, q_row0=0,
                            k_tile0=0, k_tiles=tiles_per_batch, tiles_per_batch=tiles_per_batch,
                            q_tile=DIFF_Q_TILE, lam_init=lam_init)
    w_out = bf(attn_w_out[0])
    x_lat = _ffn(x_lat, mods, norm_g[0, 2], ffn_w, 0, 1, tile=FFN_TILE, mod_base=6, group_of_tile=ffn_group,
                 attn=(a_lat, d_lat, w_out))

    a_ctx = _dense_attention(qa, ka, va, batch=batch, q_row0=batch * n_lat, k_tile0=lat_tiles,
                             tiles_per_batch=tiles_per_batch)
    d_ctx = _diff_attention(qbt, kb, vbt, diff_lambda[0], diff_subln_g[0], batch=batch, n_q=n_ctx,
                            q_row0=batch * n_lat, k_tile0=lat_tiles, k_tiles=1,
                            tiles_per_batch=tiles_per_batch, q_tile=ROW_TILE, lam_init=lam_init)
    x_ctx = _ffn(x_ctx, mods, norm_g[0, 2], ffn_w, 0, 1, tile=ROW_TILE, mod_base=6, group_of_tile=ctx_group,
                 attn=(a_ctx, d_ctx, w_out))
    del x_ctx

    mods = mods_all[1]
    x_lat = _ffn(x_lat, mods, norm_g[1, 0], ffn_w, 1, 0, tile=FFN_TILE, mod_base=0, group_of_tile=ffn_group)
    x_lat = _conv_mixer(x_lat, mods, conv_group, norm_g[1, 1], bf(conv_w_in[0]), conv_w[0], bf(conv_w_out[0]),
                        tile=LAT_TILE, seq=n_lat)
    x_lat = _ffn(x_lat, mods, norm_g[1, 2], ffn_w, 1, 1, tile=FFN_TILE, mod_base=6, group_of_tile=ffn_group,
                 final_g=final_g)
    return x_lat.reshape(batch, n_lat, d)
```

```python
import functools
import math

import jax
import jax.numpy as jnp
import numpy as np
from jax import lax
from jax.experimental import pallas as pl
from jax.experimental.pallas import tpu as pltpu

F32 = jnp.float32
BF16 = jnp.bfloat16

GRID_W = 64
HEAD_DIM = 64
NA_HEADS = 8
NA_WIN_ROWS = 8
NA_WIN_COLS = 16
DIFF_HEADS = 4
DIFF_V_DIM = 128
DIFF_V_ROWS = 144
LOG2_E = math.log2(math.e)
HALF_WIDTH = 512
ATTN_IN_WIDTH = 6 * HALF_WIDTH
CONV_WIDTH = 3
ROPE_THETA = 10000.0
N_MOD = 9
EPS = 1e-6
NEG_INF = -1e30

LANES = 128
ROW_TILE = 256
LAT_TILE = 512
FFN_TILE = 512
NA_Q_ROWS = 4
NA_BAND_ROWS = 12
DIFF_Q_TILE = 512
DIFF_K_TILE = 256
DIFF_CHUNK_TILES = 11
DIFF_UNROLL = 2
FF_CHUNKS = ((0, 1024), (1024, 1024), (2048, 768))
FFN_ROW_BLOCK = 256
FF_CAST_CHUNK = 256
VMEM_LIMIT = 56 * 1024 * 1024

NT_DIMS = (((1,), (1,)), ((), ()))


def _cparams(n_axes):
    return pltpu.CompilerParams(dimension_semantics=("arbitrary",) * n_axes,
                                vmem_limit_bytes=VMEM_LIMIT)


def _resident(shape, index_map):
    return pl.BlockSpec(shape, index_map, pipeline_mode=pl.Buffered(1))


def _modulate(x, g, shift, scale):
    ms = jnp.mean(x * x, axis=-1, keepdims=True)
    return x * lax.rsqrt(ms + EPS) * g * (1.0 + scale) + shift


def _silu(x):
    return x * (1.0 / (1.0 + jnp.exp(-x)))


def _mod_kernel(c_ref, w_ref, b_ref, o_ref):
    o_ref[0] = jnp.dot(_silu(c_ref[...]), w_ref[0], preferred_element_type=F32) + b_ref[0]


def _mod_vectors(cvec, mod_w, mod_b):
    depth, d, n = mod_w.shape
    bn = 1536
    return pl.pallas_call(
        _mod_kernel,
        grid=(depth, n // bn),
        in_specs=[pl.BlockSpec((8, d), lambda i, j: (0, 0)),
                  pl.BlockSpec((1, d, bn), lambda i, j: (i, 0, j)),
                  pl.BlockSpec((1, 1, bn), lambda i, j: (i, 0, j))],
        out_specs=pl.BlockSpec((1, 8, bn), lambda i, j: (i, 0, j)),
        out_shape=jax.ShapeDtypeStruct((depth, 8, n), F32),
        compiler_params=_cparams(2),
        name="adaln_vectors",
    )(cvec, mod_w, mod_b.reshape(depth, 1, n))


def _ffn_kernel(*refs, mod_base, attn, final, n_cast):
    refs = list(refs)
    wg_ref, wu_ref, wd_ref = refs[-3:]
    o_ref = refs[-4]
    x_ref = refs.pop(0)
    if attn:
        a_ref, d_ref, wo_ref = refs.pop(0), refs.pop(0), refs.pop(0)
    mod_ref, g_ref, wg32_ref, wu32_ref, wd32_ref = refs[:5]
    step = pl.program_id(0)

    for c in range(n_cast):
        @pl.when(step == c)
        def _(c=c):
            sl = slice(c * FF_CAST_CHUNK, (c + 1) * FF_CAST_CHUNK)
            wg_ref[:, sl] = wg32_ref[...].astype(BF16)
            wu_ref[:, sl] = wu32_ref[...].astype(BF16)
            wd_ref[sl, :] = wd32_ref[...].astype(BF16)

    @pl.when(step >= n_cast)
    def _():
        shift = mod_ref[0, mod_base:mod_base + 1, :]
        scale = mod_ref[0, mod_base + 1:mod_base + 2, :]
        gate = mod_ref[0, mod_base + 2:mod_base + 3, :]
        blocks = [slice(r, r + FFN_ROW_BLOCK) for r in range(0, x_ref.shape[0], FFN_ROW_BLOCK)]
        xs, xns = [], []
        for rs in blocks:
            x = x_ref[rs, :]
            if attn:
                y = (jnp.dot(a_ref[rs, :], wo_ref[0:HALF_WIDTH, :], preferred_element_type=F32)
                     + jnp.dot(d_ref[rs, :], wo_ref[HALF_WIDTH:2 * HALF_WIDTH, :], preferred_element_type=F32))
                x = x + mod_ref[0, 5:6, :] * y
            xs.append(x)
            xns.append(_modulate(x, g_ref[...], shift, scale).astype(BF16))
        for rs, x, xn in zip(blocks, xs, xns):
            y = None
            for lo, width in FF_CHUNKS:
                hg = jnp.dot(xn, wg_ref[:, lo:lo + width], preferred_element_type=F32)
                hu = jnp.dot(xn, wu_ref[:, lo:lo + width], preferred_element_type=F32)
                a = (_silu(hg) * hu).astype(BF16)
                part = jnp.dot(a, wd_ref[lo:lo + width, :], preferred_element_type=F32)
                y = part if y is None else y + part
            out = x + (0.5 * gate) * y
            if final:
                fg = refs[5][...]
                ms = jnp.mean(out * out, axis=-1, keepdims=True)
                out = out * lax.rsqrt(ms + EPS) * fg
            o_ref[rs, :] = out


def _ffn(x, mods, g, weights, layer, which, *, tile, mod_base, group_of_tile, attn=None, final_g=None):
    rows, d = x.shape
    n_tiles = rows // tile
    wg, wu, wd = weights
    f = wg.shape[-1]
    n_cast = f // FF_CAST_CHUNK
    row_tile = lambda s: jnp.maximum(s - n_cast, 0)
    slab = lambda s: jnp.minimum(s, n_cast - 1)
    in_specs = [pl.BlockSpec((tile, d), lambda s: (row_tile(s), 0))]
    args = [x]
    if attn is not None:
        a, dd, w_out = attn
        in_specs += [pl.BlockSpec((tile, HALF_WIDTH), lambda s: (row_tile(s), 0)),
                     pl.BlockSpec((tile, HALF_WIDTH), lambda s: (row_tile(s), 0)),
                     _resident((2 * HALF_WIDTH, d), lambda s: (0, 0))]
        args += [a, dd, w_out]
    in_specs += [pl.BlockSpec((1, N_MOD, d), lambda s: (group_of_tile(row_tile(s)), 0, 0)),
                 _resident((1, d), lambda s: (0, 0)),
                 pl.BlockSpec((None, None, d, FF_CAST_CHUNK), lambda s: (layer, which, 0, slab(s))),
                 pl.BlockSpec((None, None, d, FF_CAST_CHUNK), lambda s: (layer, which, 0, slab(s))),
                 pl.BlockSpec((None, None, FF_CAST_CHUNK, d), lambda s: (layer, which, slab(s), 0))]
    args += [mods, g.reshape(1, d), wg, wu, wd]
    if final_g is not None:
        in_specs.append(_resident((1, d), lambda s: (0, 0)))
        args.append(final_g.reshape(1, d))
    return pl.pallas_call(
        functools.partial(_ffn_kernel, mod_base=mod_base, attn=attn is not None, final=final_g is not None,
                          n_cast=n_cast),
        grid=(n_cast + n_tiles,),
        in_specs=in_specs,
        out_specs=pl.BlockSpec((tile, d), lambda s: (row_tile(s), 0)),
        out_shape=jax.ShapeDtypeStruct((rows, d), F32),
        scratch_shapes=[pltpu.VMEM((d, f), BF16), pltpu.VMEM((d, f), BF16), pltpu.VMEM((f, d), BF16)],
        compiler_params=_cparams(1),
        name="macaron_ffn",
    )(*args)


def _rope(z, cos, sin_next, sin_prev):
    return z * cos + pltpu.roll(z, LANES - 1, 1) * sin_next + pltpu.roll(z, 1, 1) * sin_prev


def _rope_tile_tables(row_ref, col_ref, is_ctx):
    grid_rows = ROW_TILE // GRID_W
    lane = lax.broadcasted_iota(jnp.int32, (ROW_TILE, LANES), 1)
    by_row = (lane % HEAD_DIM) // 2 < HEAD_DIM // 4
    tabs = []
    for k, identity in enumerate((1.0, 0.0, 0.0)):
        rows = row_ref[k, 0]
        row_part = jnp.concatenate([jnp.broadcast_to(rows[r:r + 1], (GRID_W, LANES)) for r in range(grid_rows)],
                                   axis=0)
        col_part = jnp.concatenate([col_ref[k]] * grid_rows, axis=0)
        tabs.append(jnp.where(is_ctx, identity, jnp.where(by_row, row_part, col_part)))
    return tabs


def _attn_in_kernel(x_ref, xc_ref, mod_ref, g_ref, w_ref, row_ref, col_ref,
                    qa_ref, ka_ref, va_ref, qbt_ref, kb_ref, vbt_ref, *, tiles_per_batch):
    is_ctx = pl.program_id(0) % tiles_per_batch == tiles_per_batch - 1
    x = jnp.where(is_ctx, xc_ref[...], x_ref[...])
    xn = _modulate(x, g_ref[...], mod_ref[0, 3:4, :], mod_ref[0, 4:5, :]).astype(BF16)
    w = HALF_WIDTH
    scale = HEAD_DIM ** -0.5 * LOG2_E
    cos, sn, sp = _rope_tile_tables(row_ref, col_ref, is_ctx)
    qb = jnp.dot(xn, w_ref[:, 3 * w:4 * w], preferred_element_type=F32)
    kb = jnp.dot(xn, w_ref[:, 4 * w:5 * w], preferred_element_type=F32)
    for h in range(DIFF_HEADS):
        sl = slice(h * LANES, (h + 1) * LANES)
        qbt_ref[0, sl, :] = (_rope(qb[:, sl], cos, sn, sp) * scale).T.astype(BF16)
        kb_ref[:, sl] = _rope(kb[:, sl], cos, sn, sp).astype(BF16)
    vbt = jnp.dot(xn, w_ref[:, 5 * w:6 * w], preferred_element_type=F32).T.astype(BF16)
    pad_rows = DIFF_V_ROWS - DIFF_V_DIM
    ones_row = (lax.broadcasted_iota(jnp.int32, (pad_rows, vbt.shape[1]), 0) == 0).astype(BF16)
    for h in range(DIFF_HEADS):
        vbt_ref[0, h * DIFF_V_ROWS:h * DIFF_V_ROWS + DIFF_V_DIM, :] = vbt[h * DIFF_V_DIM:(h + 1) * DIFF_V_DIM]
        vbt_ref[0, h * DIFF_V_ROWS + DIFF_V_DIM:(h + 1) * DIFF_V_ROWS, :] = ones_row
    qa_ref[...] = (jnp.dot(xn, w_ref[:, 0:w], preferred_element_type=F32) * scale).astype(BF16)
    ka_ref[...] = jnp.dot(xn, w_ref[:, w:2 * w], preferred_element_type=F32).astype(BF16)
    va_ref[...] = jnp.dot(xn, w_ref[:, 2 * w:3 * w], preferred_element_type=F32).astype(BF16)


def _rope_tables(n_lat):
    n_freq = HEAD_DIM // 4
    grid_rows = ROW_TILE // GRID_W
    lane = jnp.arange(LANES)
    pair = (lane % HEAD_DIM) // 2
    inv = ROPE_THETA ** (-(pair % n_freq).astype(F32) / n_freq)
    even = lane % 2 == 0

    def tables(pos):
        ang = pos.astype(F32)[:, None] * inv[None, :]
        sin = jnp.sin(ang)
        return jnp.stack([jnp.cos(ang), jnp.where(even, -sin, 0.0), jnp.where(even, 0.0, sin)])

    n_rows = n_lat // GRID_W
    row_tabs = tables(jnp.arange(n_rows)).reshape(3, n_rows // grid_rows, grid_rows, LANES)
    row_tabs = jnp.pad(row_tabs, ((0, 0), (0, 0), (0, 8 - grid_rows), (0, 0)))
    return row_tabs, tables(jnp.arange(GRID_W))


def _attn_in(x_lat, x_ctx, mods, g, w_in, tables, *, batch, tiles_per_batch):
    d = x_lat.shape[1]
    lat_tiles = tiles_per_batch - 1
    n_tiles = batch * tiles_per_batch
    rows = n_tiles * ROW_TILE

    def group(i):
        return jnp.where(i % tiles_per_batch == lat_tiles, batch, i // tiles_per_batch)

    def lat_first(i):
        b, j = i // tiles_per_batch, i % tiles_per_batch
        return jnp.where(j == lat_tiles, batch * lat_tiles + b, b * lat_tiles + j)

    half = jax.ShapeDtypeStruct((rows, HALF_WIDTH), BF16)
    q_spec = pl.BlockSpec((ROW_TILE, HALF_WIDTH), lambda i: (lat_first(i), 0))
    kv_spec = pl.BlockSpec((ROW_TILE, HALF_WIDTH), lambda i: (i, 0))
    def lat_tile(i):
        return (i // tiles_per_batch) * lat_tiles + jnp.minimum(i % tiles_per_batch, lat_tiles - 1)

    return pl.pallas_call(
        functools.partial(_attn_in_kernel, tiles_per_batch=tiles_per_batch),
        grid=(n_tiles,),
        in_specs=[pl.BlockSpec((ROW_TILE, d), lambda i: (lat_tile(i), 0)),
                  pl.BlockSpec((ROW_TILE, d), lambda i: (i // tiles_per_batch, 0)),
                  pl.BlockSpec((1, N_MOD, d), lambda i: (group(i), 0, 0)),
                  _resident((1, d), lambda i: (0, 0)),
                  _resident((d, ATTN_IN_WIDTH), lambda i: (0, 0)),
                  pl.BlockSpec((3, 1, 8, LANES), lambda i: (0, jnp.minimum(i % tiles_per_batch, lat_tiles - 1), 0, 0)),
                  _resident((3, GRID_W, LANES), lambda i: (0, 0, 0))],
        out_specs=[q_spec, kv_spec, kv_spec,
                   pl.BlockSpec((1, HALF_WIDTH, ROW_TILE), lambda i: (lat_first(i), 0, 0)), kv_spec,
                   pl.BlockSpec((1, DIFF_HEADS * DIFF_V_ROWS, ROW_TILE), lambda i: (i, 0, 0))],
        out_shape=[half, half, half,
                   jax.ShapeDtypeStruct((n_tiles, HALF_WIDTH, ROW_TILE), BF16), half,
                   jax.ShapeDtypeStruct((n_tiles, DIFF_HEADS * DIFF_V_ROWS, ROW_TILE), BF16)],
        compiler_params=_cparams(1),
        name="attn_in_proj",
    )(x_lat, x_ctx, mods, g.reshape(1, d), w_in, *tables)


NA_PAIR_PAD = 4


def _na_bias_pairs(rpb):
    heads = rpb.shape[0]
    qc = np.arange(GRID_W)
    col_start = np.clip(qc - NA_WIN_COLS // 2, 0, GRID_W - NA_WIN_COLS)
    col_ok = (qc[None, :] >= col_start[:, None]) & (qc[None, :] < col_start[:, None] + NA_WIN_COLS)
    rpb = rpb.astype(F32) * LOG2_E
    dc = np.clip(qc[None, :] - qc[:, None] + NA_WIN_COLS - 1, 0, 2 * NA_WIN_COLS - 2)
    onehot = (dc[None] == np.arange(2 * NA_WIN_COLS - 1)[:, None, None]) & col_ok[None]
    t = jnp.einsum('hdc,cqk->hdqk', rpb, jnp.asarray(onehot, F32), precision=lax.Precision.HIGHEST)
    t = jnp.where(col_ok[None, None], t, NEG_INF)
    t = jnp.pad(t, ((0, 0), (NA_PAIR_PAD, NA_PAIR_PAD), (0, 0), (0, 0)), constant_values=NEG_INF)
    return jnp.concatenate([t[:, :-1], t[:, 1:]], axis=-1)


def _na_fill_bias(pair_ref, bias_ref, rows):
    kr_win = min(NA_WIN_ROWS, rows)
    low = lax.broadcasted_iota(jnp.int32, (GRID_W, LANES), 1) < GRID_W
    for v, (r0, ws) in enumerate(((0, 0), (NA_Q_ROWS, 0), (rows - NA_Q_ROWS, rows - NA_BAND_ROWS))):
        for i in range(NA_Q_ROWS):
            qr = r0 + i
            start = min(max(qr - kr_win // 2, 0), rows - kr_win)
            for jp in range(NA_BAND_ROWS // 2):
                kr = ws + 2 * jp
                ok = [start <= kr + e < start + kr_win for e in range(2)]
                entry = kr - qr + NA_WIN_ROWS - 1 + NA_PAIR_PAD
                for h in range(NA_HEADS):
                    if not any(ok):
                        blk = jnp.full((GRID_W, LANES), NEG_INF, F32)
                    else:
                        blk = pair_ref[h, entry]
                        if not all(ok):
                            blk = jnp.where(low if ok[0] else jnp.logical_not(low), blk, NEG_INF)
                    bias_ref[v, h, i * GRID_W:(i + 1) * GRID_W, jp * LANES:(jp + 1) * LANES] = blk


def _na_kernel(q_ref, qn_ref, k_ref, v_ref, pair_ref, o_ref, bias_ref, sb0_ref, sc0_ref,
               *, n_lat, n_ctx, rows, groups):
    g = pl.program_id(1)

    @pl.when(jnp.logical_and(pl.program_id(0) == 0, g == 0))
    def _():
        _na_fill_bias(pair_ref, bias_ref, rows)

    band = NA_BAND_ROWS * GRID_W
    tq = NA_Q_ROWS * GRID_W
    low = lax.broadcasted_iota(jnp.int32, (tq, LANES), 1) < HEAD_DIM

    def geometry(gg):
        ws = jnp.clip(NA_Q_ROWS * gg - NA_WIN_ROWS // 2, 0, rows - NA_BAND_ROWS)
        return pl.multiple_of(ws * GRID_W, GRID_W), jnp.where(gg == 0, 0, jnp.where(gg == groups - 1, 2, 1))

    start, variant = geometry(g)
    start_n, variant_n = geometry(jnp.minimum(g + 1, groups - 1))

    def scores(h, qr=q_ref, first=None, var=None):
        first = start if first is None else first
        var = variant if var is None else var
        sl = slice(h // 2 * LANES, (h // 2 + 1) * LANES)
        q = qr[:, sl]
        qm = jnp.where(low if h % 2 == 0 else jnp.logical_not(low), q, jnp.zeros_like(q))
        sb = lax.dot_general(qm, k_ref[pl.ds(first, band), sl], NT_DIMS, preferred_element_type=F32)
        sc = lax.dot_general(qm, k_ref[n_lat:n_lat + n_ctx, sl], NT_DIMS, preferred_element_type=F32)
        return sb + bias_ref[var, h], sc

    def attend(h, sb, sc):
        sl = slice(h // 2 * LANES, (h // 2 + 1) * LANES)
        m = jnp.maximum(jnp.max(sb, axis=-1, keepdims=True), jnp.max(sc, axis=-1, keepdims=True))
        pb = jnp.exp2(sb - m)
        pc = jnp.exp2(sc - m)
        l = jnp.sum(pb, axis=-1, keepdims=True) + jnp.sum(pc, axis=-1, keepdims=True)
        o = (jnp.dot(pb.astype(BF16), v_ref[pl.ds(start, band), sl], preferred_element_type=F32)
             + jnp.dot(pc.astype(BF16), v_ref[n_lat:n_lat + n_ctx, sl], preferred_element_type=F32))
        return o * (1.0 / l)

    @pl.when(g == 0)
    def _():
        sb, sc = scores(0)
        sb0_ref[...] = sb
        sc0_ref[...] = sc

    s_cur = (sb0_ref[...], sc0_ref[...])
    even_out = None
    for h in range(NA_HEADS):
        if h + 1 < NA_HEADS:
            s_nxt = scores(h + 1)
        else:
            s_nxt = None
            sb, sc = scores(0, qn_ref, start_n, variant_n)
            sb0_ref[...] = sb
            sc0_ref[...] = sc
        o = attend(h, *s_cur)
        if h % 2 == 0:
            even_out = o
        else:
            sl = slice(h // 2 * LANES, (h // 2 + 1) * LANES)
            o_ref[:, sl] = jnp.where(low, even_out, o).astype(BF16)
        s_cur = s_nxt


def _na_attention(qa, ka, va, bias_pairs, *, batch, n_lat, n_ctx):
    rows = n_lat // GRID_W
    groups = rows // NA_Q_ROWS
    tq = NA_Q_ROWS * GRID_W
    per_batch = n_lat + n_ctx
    kv_spec = _resident((per_batch, HALF_WIDTH), lambda b, g: (b, 0))
    return pl.pallas_call(
        functools.partial(_na_kernel, n_lat=n_lat, n_ctx=n_ctx, rows=rows, groups=groups),
        grid=(batch, groups),
        in_specs=[pl.BlockSpec((tq, HALF_WIDTH), lambda b, g: (b * groups + g, 0)),
                  pl.BlockSpec((tq, HALF_WIDTH), lambda b, g: (b * groups + jnp.minimum(g + 1, groups - 1), 0)),
                  kv_spec, kv_spec,
                  _resident(bias_pairs.shape, lambda b, g: (0, 0, 0, 0))],
        out_specs=pl.BlockSpec((tq, HALF_WIDTH), lambda b, g: (b * groups + g, 0)),
        out_shape=jax.ShapeDtypeStruct((batch * n_lat, HALF_WIDTH), BF16),
        scratch_shapes=[pltpu.VMEM((3, NA_HEADS, tq, NA_BAND_ROWS * GRID_W), F32),
                        pltpu.VMEM((tq, NA_BAND_ROWS * GRID_W), F32),
                        pltpu.VMEM((tq, n_ctx), F32)],
        compiler_params=_cparams(2),
        name="neighbourhood_attention",
    )(qa, qa, ka, va, bias_pairs)


def _diff_kernel(q_ref, qn_ref, k_ref, vt_ref, lam_ref, g_ref, o_ref, qc_ref, s_ref, m_ref, alpha_ref, acc_ref,
                 *, n_chunks, tiles_per_chunk, lam_init):
    tq = q_ref.shape[0] * q_ref.shape[2]
    tk = tiles_per_chunk * DIFF_K_TILE

    def load_q(ref):
        qt = jnp.concatenate([ref[j] for j in range(ref.shape[0])], axis=1)
        low = lax.broadcasted_iota(jnp.int32, qt.shape, 0) < HEAD_DIM
        zero = jnp.zeros_like(qt)
        qc_ref[:, 0:tq] = jnp.where(low, qt, zero)
        qc_ref[:, tq:2 * tq] = jnp.where(low, zero, qt)

    def tile_scores(c, t):
        row0 = pl.multiple_of(c * tk + t * DIFF_K_TILE, DIFF_K_TILE)
        s = jnp.dot(k_ref[pl.ds(row0, DIFF_K_TILE), :], qc_ref[...], preferred_element_type=F32)
        s_ref[t * DIFF_K_TILE:(t + 1) * DIFF_K_TILE, :] = s
        return jnp.max(s, axis=0, keepdims=True)

    def finish_max(cmax, fresh):
        if fresh:
            alpha_ref[...] = jnp.zeros(alpha_ref.shape, F32)
            m_ref[...] = cmax
        else:
            m_old = m_ref[...]
            m_new = jnp.maximum(m_old, cmax)
            alpha_ref[...] = jnp.exp2(m_old - m_new)
            m_ref[...] = m_new

    def step(c, next_chunk, fresh):
        m = m_ref[...]
        alpha = alpha_ref[...]
        cmax = None
        for t in range(tiles_per_chunk):
            p = jnp.exp2(s_ref[t * DIFF_K_TILE:(t + 1) * DIFF_K_TILE, :] - m).astype(BF16)
            tmax = tile_scores(next_chunk, t)
            cmax = tmax if cmax is None else jnp.maximum(cmax, tmax)
            pv = jnp.dot(vt_ref[c * tiles_per_chunk + t], p, preferred_element_type=F32)
            acc_ref[...] = (alpha * acc_ref[...] if t == 0 else acc_ref[...]) + pv
        finish_max(cmax, fresh)

    load_q(q_ref)
    acc_ref[...] = jnp.zeros(acc_ref.shape, F32)

    @pl.when(pl.program_id(2) == 0)
    def _():
        cmax = None
        for t in range(tiles_per_chunk):
            tmax = tile_scores(0, t)
            cmax = tmax if cmax is None else jnp.maximum(cmax, tmax)
        finish_max(cmax, True)

    def body(c, carry):
        step(c, c + 1, False)
        return carry

    lax.fori_loop(0, n_chunks - 1, body, 0, unroll=DIFF_UNROLL)
    load_q(qn_ref)
    step(n_chunks - 1, 0, True)
    lv = lam_ref[...]
    lam = (jnp.exp(jnp.sum(lv[0:1] * lv[1:2], axis=-1, keepdims=True))
           - jnp.exp(jnp.sum(lv[2:3] * lv[3:4], axis=-1, keepdims=True)) + lam_init)
    o = acc_ref[0:DIFF_V_DIM, :] * (1.0 / acc_ref[DIFF_V_DIM:DIFF_V_DIM + 1, :])
    d = o[:, 0:tq] - lam * o[:, tq:2 * tq]
    ms = jnp.mean(d * d, axis=0, keepdims=True)
    d = d * lax.rsqrt(ms + EPS) * g_ref[...] * (1.0 - lam_init)
    o_ref[...] = d.T.astype(BF16)


def _diff_attention(qbt, kb, vbt, lam_vec, subln_g, *, batch, n_q, q_row0, k_tile0, k_tiles,
                    tiles_per_batch, q_tile, lam_init):
    q_tiles = n_q // q_tile
    q_blk0 = q_row0 // q_tile
    q_sub = q_tile // ROW_TILE
    tiles_per_chunk = DIFF_CHUNK_TILES if k_tiles % DIFF_CHUNK_TILES == 0 else 1
    n_chunks = k_tiles // tiles_per_chunk
    tk = tiles_per_chunk * DIFF_K_TILE
    if k_tile0 == 0:
        k_spec = _resident((k_tiles * DIFF_K_TILE, LANES), lambda b, h, i: (b, h))
    else:
        k_spec = _resident((k_tiles * DIFF_K_TILE, LANES), lambda b, h, i: (b * tiles_per_batch + k_tile0, h))
    return pl.pallas_call(
        functools.partial(_diff_kernel, n_chunks=n_chunks, tiles_per_chunk=tiles_per_chunk, lam_init=lam_init),
        grid=(batch, DIFF_HEADS, q_tiles),
        in_specs=[pl.BlockSpec((q_sub, LANES, ROW_TILE), lambda b, h, i: (q_blk0 + b * q_tiles + i, h, 0)),
                  pl.BlockSpec((q_sub, LANES, ROW_TILE),
                               lambda b, h, i: (q_blk0 + b * q_tiles + jnp.minimum(i + 1, q_tiles - 1), h, 0)),
                  k_spec,
                  _resident((k_tiles, DIFF_V_ROWS, DIFF_K_TILE),
                            lambda b, h, i: ((b * tiles_per_batch + k_tile0) // k_tiles, h, 0)),
                  _resident((4, HEAD_DIM), lambda b, h, i: (0, 0)),
                  _resident((DIFF_V_DIM, 1), lambda b, h, i: (0, 0))],
        out_specs=pl.BlockSpec((q_tile, LANES), lambda b, h, i: (b * q_tiles + i, h)),
        out_shape=jax.ShapeDtypeStruct((batch * n_q, HALF_WIDTH), BF16),
        scratch_shapes=[pltpu.VMEM((LANES, 2 * q_tile), BF16),
                        pltpu.VMEM((tk, 2 * q_tile), F32),
                        pltpu.VMEM((1, 2 * q_tile), F32),
                        pltpu.VMEM((1, 2 * q_tile), F32),
                        pltpu.VMEM((DIFF_V_ROWS, 2 * q_tile), F32)],
        compiler_params=_cparams(3),
        name="differential_attention",
    )(qbt, qbt, kb, vbt, lam_vec, subln_g.reshape(DIFF_V_DIM, 1))


def _dense_kernel(q_ref, k_ref, v_ref, o_ref):
    tq = q_ref.shape[0]
    low = lax.broadcasted_iota(jnp.int32, (tq, LANES), 1) < HEAD_DIM
    for p in range(NA_HEADS // 2):
        sl = slice(p * LANES, (p + 1) * LANES)
        q, k, v = q_ref[:, sl], k_ref[:, sl], v_ref[:, sl]
        outs = []
        for e in range(2):
            qm = jnp.where(low if e == 0 else jnp.logical_not(low), q, jnp.zeros_like(q))
            s = lax.dot_general(qm, k, NT_DIMS, preferred_element_type=F32)
            pr = jnp.exp2(s - jnp.max(s, axis=-1, keepdims=True))
            o = jnp.dot(pr.astype(BF16), v, preferred_element_type=F32)
            outs.append(o * (1.0 / jnp.sum(pr, axis=-1, keepdims=True)))
        o_ref[:, sl] = jnp.where(low, outs[0], outs[1]).astype(BF16)


def _dense_attention(qa, ka, va, *, batch, q_row0, k_tile0, tiles_per_batch):
    q_blk0 = q_row0 // ROW_TILE
    kv_spec = pl.BlockSpec((ROW_TILE, HALF_WIDTH), lambda b: (b * tiles_per_batch + k_tile0, 0))
    return pl.pallas_call(
        _dense_kernel,
        grid=(batch,),
        in_specs=[pl.BlockSpec((ROW_TILE, HALF_WIDTH), lambda b: (q_blk0 + b, 0)), kv_spec, kv_spec],
        out_specs=pl.BlockSpec((ROW_TILE, HALF_WIDTH), lambda b: (b, 0)),
        out_shape=jax.ShapeDtypeStruct((batch * ROW_TILE, HALF_WIDTH), BF16),
        compiler_params=_cparams(1),
        name="context_attention",
    )(qa, ka, va)


def _conv_kernel(x_ref, xp_ref, xs_ref, mod_ref, g_ref, w_in_ref, cw_ref, w_out_ref, o_ref, *, tiles_per_seq):
    i = pl.program_id(0)
    tile, d = x_ref.shape
    halo = xp_ref.shape[0]
    x = x_ref[...]
    x_ext = jnp.concatenate([xp_ref[...], x, xs_ref[...]], axis=0)
    xn = _modulate(x_ext, g_ref[...], mod_ref[0, 3:4, :], mod_ref[0, 4:5, :]).astype(BF16)
    cg = jnp.dot(xn, w_in_ref[:, d:2 * d], preferred_element_type=F32)
    hh = jnp.dot(xn, w_in_ref[:, 2 * d:3 * d], preferred_element_type=F32)
    u = cg * hh
    n_ext = tile + 2 * halo
    ridx = lax.broadcasted_iota(jnp.int32, (n_ext, 1), 0)
    first = (i % tiles_per_seq) == 0
    last = (i % tiles_per_seq) == tiles_per_seq - 1
    u = jnp.where(jnp.logical_or(jnp.logical_and(first, ridx < halo),
                                 jnp.logical_and(last, ridx >= halo + tile)), 0.0, u)
    cw = cw_ref[...]
    u_prev = pltpu.roll(u, 1, 0)[halo:halo + tile]
    u_next = pltpu.roll(u, n_ext - 1, 0)[halo:halo + tile]
    v = u_prev * cw[0:1] + u[halo:halo + tile] * cw[1:2] + u_next * cw[2:3]
    bg = jnp.dot(xn[halo:halo + tile], w_in_ref[:, 0:d], preferred_element_type=F32)
    y = jnp.dot((bg * v).astype(BF16), w_out_ref[...], preferred_element_type=F32)
    o_ref[...] = x + mod_ref[0, 5:6, :] * y


def _conv_mixer(x, mods, group_of_tile, g, w_in, conv_w, w_out, *, tile, seq):
    rows, d = x.shape
    halo = 8
    tiles_per_seq = seq // tile
    hb = tile // halo
    n_hblocks = rows // halo
    return pl.pallas_call(
        functools.partial(_conv_kernel, tiles_per_seq=tiles_per_seq),
        grid=(rows // tile,),
        in_specs=[pl.BlockSpec((tile, d), lambda i: (i, 0)),
                  pl.BlockSpec((halo, d), lambda i: (jnp.maximum(i * hb - 1, 0), 0)),
                  pl.BlockSpec((halo, d), lambda i: (jnp.minimum((i + 1) * hb, n_hblocks - 1), 0)),
                  pl.BlockSpec((1, N_MOD, d), lambda i: (group_of_tile(i), 0, 0)),
                  _resident((1, d), lambda i: (0, 0)),
                  _resident((d, 3 * d), lambda i: (0, 0)),
                  _resident((CONV_WIDTH, d), lambda i: (0, 0)),
                  _resident((d, d), lambda i: (0, 0))],
        out_specs=pl.BlockSpec((tile, d), lambda i: (i, 0)),
        out_shape=jax.ShapeDtypeStruct((rows, d), F32),
        compiler_params=_cparams(1),
        name="conv_mixer",
    )(x, x, x, mods, g.reshape(1, d), w_in, conv_w, w_out)


def kernel(x, c, ctx, c_ctx, mod_w, mod_b, norm_g, ffn_w_gate, ffn_w_up, ffn_w_down, attn_w_in, attn_w_out,
           na_rpb, diff_lambda, diff_subln_g, conv_w_in, conv_w_out, conv_w, final_g):
    batch, n_lat, d = x.shape
    n_ctx = ctx.shape[1]
    depth = mod_w.shape[0]
    assert n_ctx == ROW_TILE and n_lat % FFN_TILE == 0 and depth == 2 and batch + 1 <= 8
    tiles_per_batch = (n_lat + n_ctx) // ROW_TILE
    lat_tiles = n_lat // ROW_TILE
    bf = lambda a: a.astype(BF16)

    cvec = jnp.concatenate([c, c_ctx[None, :], jnp.zeros((8 - batch - 1, d), F32)], axis=0)
    mods_all = _mod_vectors(cvec, mod_w, mod_b).reshape(depth, 8, N_MOD, d)
    ffn_w = (ffn_w_gate, ffn_w_up, ffn_w_down)

    ffn_group = lambda i: i // (n_lat // FFN_TILE)
    conv_group = lambda i: i // (n_lat // LAT_TILE)
    ctx_group = lambda i: batch

    mods = mods_all[0]
    x_lat = _ffn(x.reshape(batch * n_lat, d), mods, norm_g[0, 0], ffn_w, 0, 0, tile=FFN_TILE, mod_base=0,
                 group_of_tile=ffn_group)
    x_ctx = _ffn(ctx.reshape(batch * n_ctx, d), mods, norm_g[0, 0], ffn_w, 0, 0, tile=ROW_TILE, mod_base=0,
                 group_of_tile=ctx_group)
    qa, ka, va, qbt, kb, vbt = _attn_in(x_lat, x_ctx, mods, norm_g[0, 1], bf(attn_w_in[0]),
                                        _rope_tables(n_lat), batch=batch, tiles_per_batch=tiles_per_batch)
    lam_init = 0.8 - 0.6 * math.exp(-0.3 * 0)
    a_lat = _na_attention(qa, ka, va, _na_bias_pairs(na_rpb[0]),
                          batch=batch, n_lat=n_lat, n_ctx=n_ctx)
    d_lat = _diff_attention(qbt, kb, vbt, diff_lambda[0], diff_subln_g[0], batch=batch, n_q=n_lat, q_row0=0,
                            k_tile0=0, k_tiles=tiles_per_batch, tiles_per_batch=tiles_per_batch,
                            q_tile=DIFF_Q_TILE, lam_init=lam_init)
    w_out = bf(attn_w_out[0])
    x_lat = _ffn(x_lat, mods, norm_g[0, 2], ffn_w, 0, 1, tile=FFN_TILE, mod_base=6, group_of_tile=ffn_group,
                 attn=(a_lat, d_lat, w_out))

    a_ctx = _dense_attention(qa, ka, va, batch=batch, q_row0=batch * n_lat, k_tile0=lat_tiles,
                             tiles_per_batch=tiles_per_batch)
    d_ctx = _diff_attention(qbt, kb, vbt, diff_lambda[0], diff_subln_g[0], batch=batch, n_q=n_ctx,
                            q_row0=batch * n_lat, k_tile0=lat_tiles, k_tiles=1,
                            tiles_per_batch=tiles_per_batch, q_tile=ROW_TILE, lam_init=lam_init)
    x_ctx = _ffn(x_ctx, mods, norm_g[0, 2], ffn_w, 0, 1, tile=ROW_TILE, mod_base=6, group_of_tile=ctx_group,
                 attn=(a_ctx, d_ctx, w_out))
    del x_ctx

    mods = mods_all[1]
    x_lat = _ffn(x_lat, mods, norm_g[1, 0], ffn_w, 1, 0, tile=FFN_TILE, mod_base=0, group_of_tile=ffn_group)
    x_lat = _conv_mixer(x_lat, mods, conv_group, norm_g[1, 1], bf(conv_w_in[0]), conv_w[0], bf(conv_w_out[0]),
                        tile=LAT_TILE, seq=n_lat)
    x_lat = _ffn(x_lat, mods, norm_g[1, 2], ffn_w, 1, 1, tile=FFN_TILE, mod_base=6, group_of_tile=ffn_group,
                 final_g=final_g)
    return x_lat.reshape(batch, n_lat, d)
```
